```python
import math
import jax, jax.numpy as jnp
from jax import lax
import numpy as np

D_MODEL = 1024
BATCH = 16
SEQ = 4096
DEPTH = 1
DEC_BATCH = 128
DEC_SEQ = 8
PAST_LEN = 8192
PAGE_SIZE = 128

HEAD_DIM = 64
NSA_HEADS = 8
NSA_KV = 2
NSA_HPG = NSA_HEADS // NSA_KV
SB_HEADS = 8
CMP_LEN = 32
CMP_STRIDE = 16
CMP_HID = HEAD_DIM
SEL_BLOCK = 64
SEL_TOPK = 16
WINDOW = 512
NSA_QB = 32
SB_QB = 128
D_FF = 2816
EPS = 1e-6
NEG_INF = -1.0e30
FORCE_SCORE = 1.0e6
IN_COLS = NSA_HEADS * HEAD_DIM + 6 * NSA_KV * HEAD_DIM + 3 * NSA_HEADS + 3 * SB_HEADS * HEAD_DIM + 2 * D_MODEL

kernel_name = 'nsa_stickbreak_macaron_adaln_step'


def rmsnorm(x, g):
    xf = x.astype(jnp.float32)
    n = xf * lax.rsqrt(jnp.mean(xf * xf, axis=-1, keepdims=True) + EPS)
    return (n * g.astype(jnp.float32)).astype(x.dtype)


def swiglu(h, w_in, w_out):
    a, b = jnp.split(h @ w_in, 2, axis=-1)
    return (jax.nn.silu(a) * b) @ w_out


def alibi_slopes(n):
    return 2.0 ** (-8.0 * jnp.arange(1, n + 1, dtype=jnp.float32) / n)


def masked_softmax(s, valid):
    p = jax.nn.softmax(jnp.where(valid, s, NEG_INF), axis=-1)
    return jnp.where(valid, p, 0.0)


def compress(k, pe, w1, w2):
    B, T, G, Dh = k.shape
    n_cmp = (T - CMP_LEN) // CMP_STRIDE + 1
    halves = k[:, :(n_cmp + 1) * CMP_STRIDE].reshape(B, n_cmp + 1, CMP_STRIDE, G, Dh)
    h1 = (jnp.einsum('bnlgd,lde->bnge', halves[:, :-1] + pe[None, None, :CMP_STRIDE, None, :], w1[:CMP_STRIDE])
          + jnp.einsum('bnlgd,lde->bnge', halves[:, 1:] + pe[None, None, CMP_STRIDE:, None, :], w1[CMP_STRIDE:]))
    return jax.nn.silu(h1) @ w2


def nsa_attention(q, gates, kc_raw, vc_raw, ks, vs, kw, vw, q_start, win_start, cmp_pe, cmp_w1, cmp_w2, kc_gain):
    B, Tq, H, Dh = q.shape
    Tk = ks.shape[1]
    scale = Dh ** -0.5
    slopes = alibi_slopes(NSA_HEADS).reshape(NSA_KV, NSA_HPG)
    kc = rmsnorm(compress(kc_raw, cmp_pe[0], cmp_w1[0], cmp_w2[0]), kc_gain)
    vc = compress(vc_raw, cmp_pe[1], cmp_w1[1], cmp_w2[1])
    n_cmp = kc.shape[1]
    c_start = jnp.arange(n_cmp) * CMP_STRIDE
    c_end = c_start + CMP_LEN - 1
    n_blk = -(-Tk // SEL_BLOCK)
    pad = n_blk * SEL_BLOCK - Tk
    ks_blk = jnp.pad(ks, ((0, 0), (0, pad), (0, 0), (0, 0))).reshape(B, n_blk, SEL_BLOCK, NSA_KV, Dh).transpose(0, 3, 1, 2, 4)
    vs_blk = jnp.pad(vs, ((0, 0), (0, pad), (0, 0), (0, 0))).reshape(B, n_blk, SEL_BLOCK, NSA_KV, Dh).transpose(0, 3, 1, 2, 4)
    b_start = jnp.arange(n_blk) * SEL_BLOCK
    overlap = ((c_start[:, None] + CMP_LEN > b_start[None, :]) & (c_start[:, None] < b_start[None, :] + SEL_BLOCK)).astype(jnp.float32)
    n_sel = min(SEL_TOPK, n_blk)
    kw_pad = jnp.pad(kw, ((0, 0), (WINDOW, 0), (0, 0), (0, 0)))
    vw_pad = jnp.pad(vw, ((0, 0), (WINDOW, 0), (0, 0), (0, 0)))
    qb = math.gcd(Tq, NSA_QB)
    nq = Tq // qb
    q_blocks = q.reshape(B, nq, qb, NSA_KV, NSA_HPG, Dh).swapaxes(0, 1)
    g_blocks = gates.reshape(B, nq, qb, NSA_KV, NSA_HPG, 3).swapaxes(0, 1)
    bi = jnp.arange(B)[:, None, None, None]
    gi = jnp.arange(NSA_KV)[None, None, :, None]
    wlen = WINDOW + qb - 1

    def block(args):
        qg, gb, i = args
        t0 = q_start + i * qb
        t = t0 + jnp.arange(qb)
        d_c = t[:, None] - c_end[None, :]
        s = (jnp.einsum('bqgjd,bngd->bqgjn', qg, kc).astype(jnp.float32) * scale
             - slopes[None, None, :, :, None] * d_c[None, :, None, None, :].astype(jnp.float32))
        p_c = masked_softmax(s, (d_c >= 0)[None, :, None, None, :])
        o_c = jnp.einsum('bqgjn,bngd->bqgjd', p_c.astype(vc.dtype), vc)
        imp = jnp.einsum('bqgn,nm->bqgm', p_c.sum(axis=3), overlap)
        blk_t = t // SEL_BLOCK
        j = jnp.arange(n_blk)[None, :]
        forced = (j == 0) | (j == blk_t[:, None]) | (j == blk_t[:, None] - 1)
        future = j * SEL_BLOCK > t[:, None]
        imp = jnp.where(forced[None, :, None, :], FORCE_SCORE, imp)
        imp = jnp.where(future[None, :, None, :], -1.0, imp)
        top_v, top_i = lax.top_k(imp, n_sel)
        kb = ks_blk[bi, gi, top_i]
        vb = vs_blk[bi, gi, top_i]
        kpos = top_i[..., None] * SEL_BLOCK + jnp.arange(SEL_BLOCK)
        d_s = t[None, :, None, None, None] - kpos
        valid_s = (d_s >= 0) & (top_v >= 0.0)[..., None]
        s = (jnp.einsum('bqgjd,bqgnkd->bqgjnk', qg, kb).astype(jnp.float32) * scale
             - slopes[None, None, :, :, None, None] * d_s[:, :, :, None].astype(jnp.float32))
        p_s = masked_softmax(s.reshape(B, qb, NSA_KV, NSA_HPG, n_sel * SEL_BLOCK),
                             valid_s.reshape(B, qb, NSA_KV, 1, n_sel * SEL_BLOCK))
        o_s = jnp.einsum('bqgjm,bqgmd->bqgjd', p_s.astype(vb.dtype), vb.reshape(B, qb, NSA_KV, n_sel * SEL_BLOCK, Dh))
        start = t0 + 1 - win_start
        kwb = lax.dynamic_slice_in_dim(kw_pad, start, wlen, axis=1)
        vwb = lax.dynamic_slice_in_dim(vw_pad, start, wlen, axis=1)
        kpos_w = win_start - WINDOW + start + jnp.arange(wlen)
        d_w = t[:, None] - kpos_w[None, :]
        valid_w = (d_w >= 0) & (d_w < WINDOW) & (kpos_w[None, :] >= win_start)
        s = (jnp.einsum('bqgjd,bkgd->bqgjk', qg, kwb).astype(jnp.float32) * scale
             - slopes[None, None, :, :, None] * d_w[None, :, None, None, :].astype(jnp.float32))
        p_w = masked_softmax(s, valid_w[None, :, None, None, :])
        o_w = jnp.einsum('bqgjk,bkgd->bqgjd', p_w.astype(vwb.dtype), vwb)
        return gb[..., 0:1] * o_c + gb[..., 1:2] * o_s + gb[..., 2:3] * o_w

    out = lax.map(block, (q_blocks, g_blocks, jnp.arange(nq)))
    return out.swapaxes(0, 1).reshape(B, Tq, H, Dh)


def stick_breaking(q, q_start, segments):
    B, Tq, H, Dh = q.shape
    scale = Dh ** -0.5
    outs = []
    for b0 in range(0, Tq, SB_QB):
        qb = q[:, b0:b0 + SB_QB]
        n = qb.shape[1]
        key_end = q_start + b0 + n - 1
        parts = []
        for k, v, s0 in segments:
            L = min(k.shape[1], key_end - s0)
            if L > 0:
                full = L == k.shape[1]
                parts.append((k if full else k[:, :L], v if full else v[:, :L], s0, L))
        z = jnp.concatenate([jnp.einsum('bqhd,bkhd->bhqk', qb, kk) for kk, _, _, _ in parts], axis=-1).astype(jnp.float32) * scale
        kpos = jnp.concatenate([s0 + jnp.arange(L) for _, _, s0, L in parts])
        qpos = q_start + b0 + jnp.arange(n)
        causal = kpos[None, :] < qpos[:, None]
        sp = jnp.where(causal, jax.nn.softplus(z), 0.0)
        later = lax.cumsum(sp, axis=3, reverse=True) - sp
        a = jnp.where(causal, jnp.exp(jax.nn.log_sigmoid(z) - later), 0.0)
        contribs = []
        off = 0
        for _, vv, _, L in parts:
            contribs.append(jnp.einsum('bhqk,bkhd->bqhd', a[..., off:off + L].astype(vv.dtype), vv))
            off += L
        outs.append(sum(contribs[1:], contribs[0]))
    return jnp.concatenate(outs, axis=1)


def mixer(h, past, w_in, q_gain, k_gain, cmp_pe, cmp_w1, cmp_w2, w_up_nsa, w_up_sb, w_out):
    B, T, _ = h.shape
    sizes = ([NSA_HEADS * HEAD_DIM] + [NSA_KV * HEAD_DIM] * 6 + [NSA_HEADS * 3]
             + [SB_HEADS * HEAD_DIM] * 3 + [D_MODEL, D_MODEL])
    offsets = np.cumsum(sizes)[:-1].tolist()
    q, kc, vc, ks, vs, kw, vw, g_nsa, sq, sk, sv, g_mn, g_ms = jnp.split(h @ w_in, offsets, axis=-1)
    nsa_h = lambda a: a.reshape(B, T, NSA_KV, HEAD_DIM)
    sb_h = lambda a: a.reshape(B, T, SB_HEADS, HEAD_DIM)
    q = rmsnorm(q.reshape(B, T, NSA_HEADS, HEAD_DIM), q_gain)
    kc, vc = nsa_h(kc), nsa_h(vc)
    ks, vs = rmsnorm(nsa_h(ks), k_gain[1]), nsa_h(vs)
    kw, vw = rmsnorm(nsa_h(kw), k_gain[2]), nsa_h(vw)
    sq, sk, sv = sb_h(sq), sb_h(sk), sb_h(sv)
    g_nsa = jax.nn.sigmoid(g_nsa.reshape(B, T, NSA_HEADS, 3))
    new_cmp = jnp.stack([kc, vc], axis=2)
    new_sel = jnp.stack([ks, vs], axis=2)
    new_sb = jnp.stack([sk, sv], axis=2)
    win_rows = jnp.stack([kw, vw], axis=2)
    if past is None:
        q_start = 0
        kc_all, vc_all, ks_all, vs_all = kc, vc, ks, vs
        sb_segments = ((sk, sv, 0),)
        win_all, win_start = win_rows, 0
    else:
        kc_p, vc_p, ks_p, vs_p, sk_p, sv_p, win_buf = past
        q_start = kc_p.shape[1]
        kc_all = jnp.concatenate([kc_p, kc], axis=1)
        vc_all = jnp.concatenate([vc_p, vc], axis=1)
        ks_all = jnp.concatenate([ks_p, ks], axis=1)
        vs_all = jnp.concatenate([vs_p, vs], axis=1)
        sb_segments = ((sk_p, sv_p, 0), (sk, sv, q_start))
        win_all = jnp.concatenate([win_buf, win_rows], axis=1)
        win_start = q_start - win_buf.shape[1]
    new_win = win_all[:, win_all.shape[1] - min(WINDOW, win_all.shape[1]):]
    o_nsa = nsa_attention(q, g_nsa, kc_all, vc_all, ks_all, vs_all, win_all[:, :, 0], win_all[:, :, 1],
                          q_start, win_start, cmp_pe, cmp_w1, cmp_w2, k_gain[0])
    o_sb = stick_breaking(sq, q_start, sb_segments)
    y = (jax.nn.sigmoid(g_mn) * (o_nsa.reshape(B, T, NSA_HEADS * HEAD_DIM) @ w_up_nsa)
         + jax.nn.sigmoid(g_ms) * (o_sb.reshape(B, T, SB_HEADS * HEAD_DIM) @ w_up_sb))
    return y @ w_out, new_cmp, new_sel, new_sb, new_win


def trunk_layer(x, c, past, lw):
    (n1, nm, n2, w_ada, b_ada, f1_in, f1_out, f2_in, f2_out, w_in, q_gain, k_gain,
     cmp_pe, cmp_w1, cmp_w2, w_up_nsa, w_up_sb, w_out) = lw
    mod = (jax.nn.silu(c) @ w_ada + b_ada)[:, None, :]
    sh1, sc1, gt1, sh2, sc2, gt2, sh3, sc3, gt3 = jnp.split(mod, 9, axis=-1)
    x = x + 0.5 * gt1 * swiglu(rmsnorm(x, n1) * (1.0 + sc1) + sh1, f1_in, f1_out)
    m, new_cmp, new_sel, new_sb, new_win = mixer(rmsnorm(x, nm) * (1.0 + sc2) + sh2, past, w_in, q_gain, k_gain,
                                                cmp_pe, cmp_w1, cmp_w2, w_up_nsa, w_up_sb, w_out)
    x = x + gt2 * m
    x = x + 0.5 * gt3 * swiglu(rmsnorm(x, n2) * (1.0 + sc3) + sh3, f2_in, f2_out)
    return x, new_cmp, new_sel, new_sb, new_win


def gather_past(pool, layer, page_table, kv):
    rows = pool[layer, page_table, :, kv]
    return rows.reshape(rows.shape[0], rows.shape[1] * rows.shape[2], rows.shape[3], rows.shape[4])


def setup_inputs(seed: int = 0) -> dict:
    key = jax.random.key(seed)
    k = jax.random.split(key, 32)
    f32 = jnp.float32

    def nrm(kk, shape, s=1.0):
        return s * jax.random.normal(kk, shape, f32)

    n_pages = PAST_LEN // PAGE_SIZE
    n_used = DEC_BATCH * n_pages
    n_pool = (n_used * 5) // 4
    win_buf = min(WINDOW, PAST_LEN)
    nsa_w = NSA_HEADS * HEAD_DIM
    sb_w = SB_HEADS * HEAD_DIM
    return {
        'x_prompt': nrm(k[0], (BATCH, SEQ, D_MODEL)),
        'x_sample': nrm(k[1], (DEC_BATCH, DEC_SEQ, D_MODEL)),
        'c_prompt': nrm(k[2], (BATCH, D_MODEL)),
        'c_sample': nrm(k[3], (DEC_BATCH, D_MODEL)),
        'cache_nsa_cmp': nrm(k[4], (DEPTH, n_pool, PAGE_SIZE, 2, NSA_KV, HEAD_DIM)),
        'cache_nsa_sel': nrm(k[5], (DEPTH, n_pool, PAGE_SIZE, 2, NSA_KV, HEAD_DIM)),
        'cache_sb': nrm(k[6], (DEPTH, n_pool, PAGE_SIZE, 2, SB_HEADS, HEAD_DIM)),
        'state_nsa_win': nrm(k[7], (DEPTH, DEC_BATCH, win_buf, 2, NSA_KV, HEAD_DIM)),
        'page_table': jax.random.permutation(k[8], n_pool)[:n_used].reshape(DEC_BATCH, n_pages).astype(jnp.int32),
        'norm_ffn1': 1.0 + nrm(k[9], (DEPTH, D_MODEL), 0.02),
        'norm_mix': 1.0 + nrm(k[10], (DEPTH, D_MODEL), 0.02),
        'norm_ffn2': 1.0 + nrm(k[11], (DEPTH, D_MODEL), 0.02),
        'w_ada': nrm(k[12], (DEPTH, D_MODEL, 9 * D_MODEL), D_MODEL ** -0.5),
        'b_ada': nrm(k[13], (DEPTH, 9 * D_MODEL), 0.02),
        'ffn1_w_in': nrm(k[14], (DEPTH, D_MODEL, 2 * D_FF), D_MODEL ** -0.5),
        'ffn1_w_out': nrm(k[15], (DEPTH, D_FF, D_MODEL), D_FF ** -0.5),
        'ffn2_w_in': nrm(k[16], (DEPTH, D_MODEL, 2 * D_FF), D_MODEL ** -0.5),
        'ffn2_w_out': nrm(k[17], (DEPTH, D_FF, D_MODEL), D_FF ** -0.5),
        'w_in': nrm(k[18], (DEPTH, D_MODEL, IN_COLS), D_MODEL ** -0.5),
        'nsa_q_norm': 1.0 + nrm(k[19], (DEPTH, HEAD_DIM), 0.02),
        'nsa_k_norm': 1.0 + nrm(k[20], (DEPTH, 3, HEAD_DIM), 0.02),
        'cmp_pe': nrm(k[21], (DEPTH, 2, CMP_LEN, HEAD_DIM), 0.1),
        'cmp_w1': nrm(k[22], (DEPTH, 2, CMP_LEN, HEAD_DIM, CMP_HID), (CMP_LEN * HEAD_DIM) ** -0.5),
        'cmp_w2': nrm(k[23], (DEPTH, 2, CMP_HID, HEAD_DIM), CMP_HID ** -0.5),
        'w_up_nsa': nrm(k[24], (DEPTH, nsa_w, D_MODEL), nsa_w ** -0.5),
        'w_up_sb': nrm(k[25], (DEPTH, sb_w, D_MODEL), sb_w ** -0.5),
        'w_out': nrm(k[26], (DEPTH, D_MODEL, D_MODEL), D_MODEL ** -0.5),
    }


def reference(x_prompt, x_sample, c_prompt, c_sample, cache_nsa_cmp, cache_nsa_sel, cache_sb, state_nsa_win,
              page_table, norm_ffn1, norm_mix, norm_ffn2, w_ada, b_ada, ffn1_w_in, ffn1_w_out, ffn2_w_in,
              ffn2_w_out, w_in, nsa_q_norm, nsa_k_norm, cmp_pe, cmp_w1, cmp_w2, w_up_nsa, w_up_sb, w_out):
    y_prompt, y_sample = x_prompt, x_sample
    rows_p, rows_s = [], []
    for l in range(DEPTH):
        lw = (norm_ffn1[l], norm_mix[l], norm_ffn2[l], w_ada[l], b_ada[l], ffn1_w_in[l], ffn1_w_out[l],
              ffn2_w_in[l], ffn2_w_out[l], w_in[l], nsa_q_norm[l], nsa_k_norm[l], cmp_pe[l], cmp_w1[l],
              cmp_w2[l], w_up_nsa[l], w_up_sb[l], w_out[l])
        y_prompt, *rp = trunk_layer(y_prompt, c_prompt, None, lw)
        past = (gather_past(cache_nsa_cmp, l, page_table, 0), gather_past(cache_nsa_cmp, l, page_table, 1),
                gather_past(cache_nsa_sel, l, page_table, 0), gather_past(cache_nsa_sel, l, page_table, 1),
                gather_past(cache_sb, l, page_table, 0), gather_past(cache_sb, l, page_table, 1),
                state_nsa_win[l])
        y_sample, *rs = trunk_layer(y_sample, c_sample, past, lw)
        rows_p.append(rp)
        rows_s.append(rs)
    new_cmp_prompt = jnp.stack([r[0] for r in rows_p])
    new_sel_prompt = jnp.stack([r[1] for r in rows_p])
    new_sb_prompt = jnp.stack([r[2] for r in rows_p])
    new_win_prompt = jnp.stack([r[3] for r in rows_p])
    new_cmp_sample = jnp.stack([r[0] for r in rows_s])
    new_sel_sample = jnp.stack([r[1] for r in rows_s])
    new_sb_sample = jnp.stack([r[2] for r in rows_s])
    new_win_sample = jnp.stack([r[3] for r in rows_s])
    return (y_prompt, y_sample, new_cmp_prompt, new_sel_prompt, new_sb_prompt, new_win_prompt,
            new_cmp_sample, new_sel_sample, new_sb_sample, new_win_sample)
```

```python
import functools

import numpy as np
import jax
import jax.numpy as jnp
from jax import lax
from jax.experimental import pallas as pl
from jax.experimental.pallas import tpu as pltpu

F32 = jnp.float32
BF16 = jnp.bfloat16

D_MODEL = 1024
HEAD_DIM = 64
NSA_HEADS = 8
NSA_KV = 2
SB_HEADS = 8
CMP_LEN = 32
CMP_STRIDE = 16
SEL_BLOCK = 64
SEL_TOPK = 16
WINDOW = 512
D_FF = 2816
EPS = 1e-6
NEG_INF = -1.0e30
FORCE_SCORE = 1.0e6
PAGE = 128

LANES = 128
KEY_BLOCK = 128
VMEM_LIMIT = 56 * 1024 * 1024
ATTN_SCALE = HEAD_DIM ** -0.5

_QPERM = np.concatenate([np.concatenate([np.arange(64 * i, 64 * i + 64), np.arange(64 * (4 + i), 64 * (4 + i) + 64)])
                         for i in range(4)])
_C_Q, _C_SQ, _C_GMN, _C_GMS, _C_GN, _C_END = 0, 512, 1024, 2048, 3072, 3200
_R_CMP, _R_SEL, _R_WIN, _R_SB, _R_END = 0, 256, 512, 768, 1792


def _params(sem, vmem=VMEM_LIMIT):
    return pltpu.CompilerParams(dimension_semantics=sem, vmem_limit_bytes=vmem)


def _dot(a, b):
    return jnp.dot(a, b, preferred_element_type=F32)


def _dot_nt(a, b):
    return lax.dot_general(a, b, (((1,), (1,)), ((), ())), preferred_element_type=F32)


def _split_bf16(x):
    hi = x.astype(BF16)
    lo = (x - hi.astype(F32)).astype(BF16)
    return hi, lo


def _silu(x):
    return x * jax.nn.sigmoid(x)


def _norm_pairs(x, gain):
    outs = []
    for c in range(x.shape[1] // LANES):
        xt = x[:, c * LANES:(c + 1) * LANES]
        lo = lax.broadcasted_iota(jnp.int32, xt.shape, 1) < HEAD_DIM
        sq = xt * xt
        ms_lo = jnp.sum(jnp.where(lo, sq, 0.0), axis=-1, keepdims=True) * (1.0 / HEAD_DIM)
        ms_hi = jnp.sum(jnp.where(lo, 0.0, sq), axis=-1, keepdims=True) * (1.0 / HEAD_DIM)
        inv = jnp.where(lo, lax.rsqrt(ms_lo + EPS), lax.rsqrt(ms_hi + EPS))
        outs.append(xt * inv * gain)
    return outs[0] if len(outs) == 1 else jnp.concatenate(outs, axis=1)


def _ada_norm(x, gain, shift, scale):
    ms = jnp.mean(x * x, axis=-1, keepdims=True)
    n = x * lax.rsqrt(ms + EPS) * gain
    h = n * (1.0 + scale) + shift
    return h.reshape(x.shape[0] * x.shape[1], x.shape[2]).astype(BF16)


def _mod_kernel(c_ref, w_ref, b_ref, o_ref):
    s = _silu(c_ref[...]).astype(BF16)
    o_ref[...] = _dot(s, w_ref[...].astype(BF16)) + b_ref[...]


def _modulation(c, w_ada, b_ada):
    rows = c.shape[0]
    n_out = w_ada.shape[1]
    tn = D_MODEL
    return pl.pallas_call(
        _mod_kernel,
        grid=(n_out // tn,),
        in_specs=[pl.BlockSpec((rows, D_MODEL), lambda j: (0, 0)),
                  pl.BlockSpec((D_MODEL, tn), lambda j: (0, j)),
                  pl.BlockSpec((1, tn), lambda j: (0, j))],
        out_specs=pl.BlockSpec((rows, tn), lambda j: (0, j)),
        out_shape=jax.ShapeDtypeStruct((rows, n_out), F32),
        compiler_params=_params(("arbitrary",)),
        name="adaln_modulation",
    )(c, w_ada, b_ada.reshape(1, n_out))


def _ffn_kernel(x_ref, sh_ref, sc_ref, gt_ref, g_ref, wi_ref, wo_ref, o_ref, *, chunk):
    x = x_ref[...]
    hb = _ada_norm(x, g_ref[...], sh_ref[...], sc_ref[...])
    acc = None
    for c0 in range(0, D_FF, chunk):
        a = _dot(hb, wi_ref[:, c0:c0 + chunk])
        b = _dot(hb, wi_ref[:, D_FF + c0:D_FF + c0 + chunk])
        act = (_silu(a) * b).astype(BF16)
        part = _dot(act, wo_ref[c0:c0 + chunk, :])
        acc = part if acc is None else acc + part
    o_ref[...] = x + 0.5 * gt_ref[...] * acc.reshape(x.shape)


def _row_tiles(batch, tokens, rows=512):
    if tokens >= rows:
        return 1, rows
    bb = max(1, min(batch, rows // tokens))
    while batch % bb:
        bb -= 1
    return bb, tokens


def _mod_spec(bb, k):
    return pl.BlockSpec((bb, 1, D_MODEL), lambda i, j, k=k: (i, 0, k))


def _const_spec(shape):
    nd = len(shape)
    return pl.BlockSpec(shape, lambda i, j: (0,) * nd, pipeline_mode=pl.Buffered(1))


def _ffn(x, mod, k0, gain, w_in, w_out):
    batch, tokens, _ = x.shape
    bb, tt = _row_tiles(batch, tokens)
    xspec = pl.BlockSpec((bb, tt, D_MODEL), lambda i, j: (i, j, 0))
    return pl.pallas_call(
        functools.partial(_ffn_kernel, chunk=D_FF // 2),
        grid=(batch // bb, tokens // tt),
        in_specs=[xspec, _mod_spec(bb, k0), _mod_spec(bb, k0 + 1), _mod_spec(bb, k0 + 2),
                  _const_spec((1, D_MODEL)), _const_spec((D_MODEL, 2 * D_FF)), _const_spec((D_FF, D_MODEL))],
        out_specs=xspec,
        out_shape=jax.ShapeDtypeStruct(x.shape, F32),
        compiler_params=_params(("parallel", "parallel")),
        name="ffn_swiglu",
    )(x, mod, mod, mod, gain.reshape(1, D_MODEL), w_in, w_out)


def _norm_groups_t(x, gain_col):
    outs = []
    for g in range(NSA_KV):
        xg = x[g * HEAD_DIM:(g + 1) * HEAD_DIM]
        ms = jnp.mean(xg * xg, axis=0, keepdims=True)
        outs.append(xg * lax.rsqrt(ms + EPS) * gain_col[g * HEAD_DIM:(g + 1) * HEAD_DIM])
    return jnp.concatenate(outs, axis=0)


def _proj_kernel(x_ref, sh_ref, sc_ref, g_ref, w_ref, wt_ref, qg_ref, kg_ref,
                 q_o, sq_o, gmn_o, gms_o, gn_o, cmp_o, sel_o, win_o, sb_o):
    x = x_ref[...]
    bb, tt, _ = x.shape
    hb = _ada_norm(x, g_ref[...], sh_ref[...], sc_ref[...])

    def cols(c0, c1):
        return _dot(hb, w_ref[:, c0:c1])

    def rows_t(r0, r1):
        return _dot_nt(wt_ref[r0:r1, :], hb)

    def put(ref, val):
        ref[...] = val.reshape(bb, tt, val.shape[-1])

    put(q_o, _norm_pairs(cols(_C_Q, _C_SQ), qg_ref[...]))
    put(sq_o, cols(_C_SQ, _C_GMN))
    put(gmn_o, jax.nn.sigmoid(cols(_C_GMN, _C_GMS)))
    put(gms_o, jax.nn.sigmoid(cols(_C_GMS, _C_GN)))
    put(gn_o, jax.nn.sigmoid(cols(_C_GN, _C_END)))
    cmp_o[0] = rows_t(_R_CMP, _R_SEL)
    sel_o[0, 0:LANES, :] = _norm_groups_t(rows_t(_R_SEL, _R_SEL + LANES), kg_ref[0])
    sel_o[0, LANES:2 * LANES, :] = rows_t(_R_SEL + LANES, _R_WIN)
    win_o[0, 0:LANES, :] = _norm_groups_t(rows_t(_R_WIN, _R_WIN + LANES), kg_ref[1])
    win_o[0, LANES:2 * LANES, :] = rows_t(_R_WIN + LANES, _R_SB)
    sb_o[0] = rows_t(_R_SB, _R_END)


def _projection(x, mod, gain, w_perm, w_kv_t, q_gain, k_gain2):
    batch, tokens, _ = x.shape
    bb, tt = _row_tiles(batch, tokens)
    widths = (512, 512, 1024, 1024, 128)
    feats = (256, 256, 256, 1024)

    def rspec(w):
        return pl.BlockSpec((bb, tt, w), lambda i, j: (i, j, 0))

    if bb == 1:
        t_shape = lambda f: (batch, f, tokens)
        tspec = lambda f: pl.BlockSpec((1, f, tt), lambda i, j: (i, 0, j))
    else:
        assert tt == tokens
        t_shape = lambda f: (1, f, batch * tokens)
        tspec = lambda f: pl.BlockSpec((1, f, bb * tt), lambda i, j: (0, 0, i))

    return pl.pallas_call(
        _proj_kernel,
        grid=(batch // bb, tokens // tt),
        in_specs=[rspec(D_MODEL), _mod_spec(bb, 3), _mod_spec(bb, 4),
                  _const_spec((1, D_MODEL)), _const_spec((D_MODEL, _C_END)), _const_spec((_R_END, D_MODEL)),
                  _const_spec((1, LANES)), _const_spec((2, LANES, 1))],
        out_specs=[rspec(w) for w in widths] + [tspec(f) for f in feats],
        out_shape=([jax.ShapeDtypeStruct((batch, tokens, w), F32) for w in widths]
                   + [jax.ShapeDtypeStruct(t_shape(f), F32) for f in feats]),
        compiler_params=_params(("parallel", "parallel")),
        name="mixer_in_proj",
    )(x, mod, mod, gain.reshape(1, D_MODEL), w_perm, w_kv_t, q_gain, k_gain2)


def _merge_kernel(x_ref, gt_ref, on_ref, os_ref, gmn_ref, gms_ref, wun_ref, wus_ref, wo_ref, o_ref):
    x = x_ref[...]
    rows = x.shape[0] * x.shape[1]

    def flat(ref):
        v = ref[...]
        return v.reshape(rows, v.shape[-1])

    y = (flat(gmn_ref) * _dot(flat(on_ref).astype(BF16), wun_ref[...])
         + flat(gms_ref) * _dot(flat(os_ref).astype(BF16), wus_ref[...]))
    m = _dot(y.astype(BF16), wo_ref[...])
    o_ref[...] = x + gt_ref[...] * m.reshape(x.shape)


def _merge(x, mod, o_nsa, o_sb, g_mn, g_ms, w_up_nsa, w_up_sb, w_out):
    batch, tokens, _ = x.shape
    bb, tt = _row_tiles(batch, tokens)

    def rspec(w):
        return pl.BlockSpec((bb, tt, w), lambda i, j: (i, j, 0))

    return pl.pallas_call(
        _merge_kernel,
        grid=(batch // bb, tokens // tt),
        in_specs=[rspec(D_MODEL), _mod_spec(bb, 5), rspec(512), rspec(512), rspec(D_MODEL), rspec(D_MODEL),
                  _const_spec((512, D_MODEL)), _const_spec((512, D_MODEL)), _const_spec((D_MODEL, D_MODEL))],
        out_specs=rspec(D_MODEL),
        out_shape=jax.ShapeDtypeStruct(x.shape, F32),
        compiler_params=_params(("parallel", "parallel")),
        name="mixer_out_merge",
    )(x, mod, o_nsa, o_sb, g_mn, g_ms, w_up_nsa, w_up_sb, w_out)


def _gather_kernel(pt_ref, cache_ref, *rest, n_pages, has_tail):
    if has_tail:
        tail_ref, o_ref, sems = rest
    else:
        o_ref, sems = rest
    b = pl.program_id(0)

    def page_copy(p):
        return pltpu.make_async_copy(cache_ref.at[pt_ref[b, p]], o_ref.at[b, :, pl.ds(p * PAGE, PAGE)], sems.at[p])

    def tail_copy():
        return pltpu.make_async_copy(tail_ref.at[b], o_ref.at[b, :, pl.ds(n_pages * PAGE, PAGE)], sems.at[n_pages])

    def start(p, carry):
        page_copy(p).start()
        return carry

    def wait(p, carry):
        page_copy(p).wait()
        return carry

    lax.fori_loop(0, n_pages, start, 0)
    if has_tail:
        tail_copy().start()
    lax.fori_loop(0, n_pages, wait, 0)
    if has_tail:
        tail_copy().wait()


def _gather_pages(cache, page_table, tail):
    dbatch, n_pages = page_table.shape
    width = cache.shape[1]
    has_tail = tail is not None
    cols = n_pages * PAGE + (PAGE if has_tail else 0)
    any_spec = pl.BlockSpec(memory_space=pl.ANY)
    operands = (page_table, cache) + ((tail,) if has_tail else ())
    return pl.pallas_call(
        functools.partial(_gather_kernel, n_pages=n_pages, has_tail=has_tail),
        grid_spec=pltpu.PrefetchScalarGridSpec(
            num_scalar_prefetch=1,
            grid=(dbatch,),
            in_specs=[any_spec] * (len(operands) - 1),
            out_specs=any_spec,
            scratch_shapes=[pltpu.SemaphoreType.DMA((n_pages + 1,))]),
        out_shape=jax.ShapeDtypeStruct((dbatch, width, cols), F32),
        compiler_params=_params(("arbitrary",)),
        name="page_gather",
    )(*operands)


def _cmp_kernel(x_ref, pe_ref, w1_ref, w2_ref, kg_ref, o_ref, tok_ref):
    n_tok = x_ref.shape[2]
    rows = n_tok // CMP_STRIDE

    def to_token_major(c, carry):
        c0 = pl.multiple_of(c * LANES, LANES)
        xt = x_ref[0, :, pl.ds(c0, LANES)].T
        tok_ref[0, pl.ds(c0, LANES), :] = xt[:, 0:LANES]
        tok_ref[1, pl.ds(c0, LANES), :] = xt[:, LANES:2 * LANES]
        return carry

    lax.fori_loop(0, n_tok // LANES, to_token_major, 0)
    for kv in range(2):
        x = jnp.concatenate([tok_ref[kv, pl.ds(l, rows, stride=CMP_STRIDE), :] for l in range(CMP_STRIDE)], axis=1)
        a = _dot((x + pe_ref[2 * kv:2 * kv + 1, :]).astype(BF16), w1_ref[2 * kv])
        b = _dot((x + pe_ref[2 * kv + 1:2 * kv + 2, :]).astype(BF16), w1_ref[2 * kv + 1])
        h1 = a + pltpu.roll(b, rows - 1, axis=0)
        c = _dot(_silu(h1).astype(BF16), w2_ref[kv])
        o_ref[0, :, kv * LANES:(kv + 1) * LANES] = _norm_pairs(c, kg_ref[...]) if kv == 0 else c


def _compress(x_t, pe4, w1_big, w2_big, k_gain):
    batch, feats, n_tok = x_t.shape
    rows = n_tok // CMP_STRIDE
    width = CMP_STRIDE * LANES
    return pl.pallas_call(
        _cmp_kernel,
        grid=(batch,),
        in_specs=[pl.BlockSpec((1, feats, n_tok), lambda b: (b, 0, 0)),
                  pl.BlockSpec((4, width), lambda b: (0, 0)),
                  pl.BlockSpec((4, width, LANES), lambda b: (0, 0, 0)),
                  pl.BlockSpec((2, LANES, LANES), lambda b: (0, 0, 0)),
                  pl.BlockSpec((1, LANES), lambda b: (0, 0))],
        out_specs=pl.BlockSpec((1, rows, 256), lambda b: (b, 0, 0)),
        out_shape=jax.ShapeDtypeStruct((batch, rows, 256), F32),
        scratch_shapes=[pltpu.VMEM((2, n_tok, LANES), F32)],
        compiler_params=_params(("parallel",)),
        name="nsa_compress",
    )(x_t, pe4, w1_big, w2_big, k_gain)


def _nsa_kernel(q_ref, g_ref, c_ref, ks_ref, vs_ref, kw_ref, vw_ref, o_ref,
                acc_ref, m_ref, l_ref, out_ref, imp_ref,
                *, tq, q_start, win_start, n_cmp, n_blk):
    ncp = c_ref.shape[1]
    nbp = imp_ref.shape[0]
    tqs = imp_ref.shape[1] // 2
    i = pl.program_id(1)
    t0 = q_start + i * tq
    slopes = [2.0 ** (-(h + 1)) for h in range(NSA_HEADS)]

    lane = lax.broadcasted_iota(jnp.int32, (tq, LANES), 1)
    lo = lane < HEAD_DIM
    t_col = t0 + lax.broadcasted_iota(jnp.int32, (tq, LANES), 0)

    tiles = [q_ref[0, :, ti * LANES:(ti + 1) * LANES] * ATTN_SCALE for ti in range(4)]
    qs = jnp.concatenate([jnp.where(lo, t, 0.0) for t in tiles] + [jnp.where(lo, 0.0, t) for t in tiles],
                         axis=0).astype(BF16)
    gates = g_ref[0]

    def gate(r, branch):
        return gates[:, 3 * r + branch:3 * r + branch + 1]

    def rows(r):
        return slice(r * tq, (r + 1) * tq)

    kc = c_ref[0, :, 0:LANES].astype(BF16)
    vc = c_ref[0, :, LANES:2 * LANES].astype(BF16)
    s_c = _dot_nt(qs, kc)
    n_idx = lax.broadcasted_iota(jnp.int32, (tq, ncp), 1)
    d_c = (t0 + lax.broadcasted_iota(jnp.int32, (tq, ncp), 0)) - (CMP_STRIDE * n_idx + (CMP_LEN - 1))
    valid_c = (d_c >= 0) & (n_idx < n_cmp)
    d_cf = d_c.astype(F32)
    probs = []
    for r in range(NSA_HEADS):
        s = jnp.where(valid_c, s_c[rows(r)] - slopes[r] * d_cf, NEG_INF)
        m = jnp.max(s, axis=-1, keepdims=True)
        e = jnp.where(valid_c, jnp.exp(s - m), 0.0)
        l = jnp.sum(e, axis=-1, keepdims=True)
        probs.append(e / jnp.where(l > 0.0, l, 1.0))
    o_c = _dot(jnp.concatenate(probs, axis=0).astype(BF16), vc)
    for r in range(NSA_HEADS):
        out_ref[rows(r), :] = gate(r, 0) * o_c[rows(r)]

    psum = []
    for g in range(NSA_KV):
        ps = probs[4 * g] + probs[4 * g + 1] + probs[4 * g + 2] + probs[4 * g + 3]
        if tqs > tq:
            ps = jnp.concatenate([ps, jnp.zeros((tqs - tq, ncp), F32)], axis=0)
        psum.append(ps)
    p_hi, p_lo = _split_bf16(jnp.concatenate(psum, axis=0))
    m_o = lax.broadcasted_iota(jnp.int32, (nbp, ncp), 0)
    n_o = lax.broadcasted_iota(jnp.int32, (nbp, ncp), 1)
    ov_t = jnp.where((n_o >= 4 * m_o - 1) & (n_o <= 4 * m_o + 3) & (n_o < n_cmp), 1.0, 0.0).astype(BF16)
    imp = _dot_nt(ov_t, p_hi) + _dot_nt(ov_t, p_lo)
    m_i = lax.broadcasted_iota(jnp.int32, (nbp, 2 * tqs), 0)
    q_i = lax.broadcasted_iota(jnp.int32, (nbp, 2 * tqs), 1)
    t_row = t0 + jnp.where(q_i >= tqs, q_i - tqs, q_i)
    blk_t = t_row >> 6
    forced = (m_i == 0) | (m_i == blk_t) | (m_i == blk_t - 1)
    future = (m_i * SEL_BLOCK > t_row) | (m_i >= n_blk)
    imp = jnp.where(future, -1.0, jnp.where(forced, FORCE_SCORE, imp))
    imp_ref[...] = imp

    def rank_body(j, cnt):
        row = imp_ref[pl.ds(j, 1), :]
        tie = jnp.where(m_i > j, 1.0, 0.0)
        return cnt + jnp.where(row > imp, 1.0, jnp.where(row == imp, tie, 0.0))

    cnt = lax.fori_loop(0, n_blk, rank_body, jnp.zeros((nbp, 2 * tqs), F32))
    sel_t = jnp.where(cnt < float(SEL_TOPK), jnp.where(imp >= 0.0, 1.0, 0.0), 0.0).astype(BF16)
    e_r = lax.broadcasted_iota(jnp.int32, (2 * tqs, 2 * tqs), 0)
    e_c = lax.broadcasted_iota(jnp.int32, (2 * tqs, 2 * tqs), 1)
    eye = jnp.where(e_r == e_c, 1.0, 0.0).astype(BF16)
    sel = _dot_nt(eye, sel_t).astype(BF16)

    def reset():
        acc_ref[...] = jnp.zeros_like(acc_ref)
        l_ref[...] = jnp.zeros_like(l_ref)
        m_ref[...] = jnp.full_like(m_ref, NEG_INF)

    def attend(k_t, v_t, d_f, valid_of):
        s_all = _dot(qs, k_t.astype(BF16))
        es = []
        for r in range(NSA_HEADS):
            valid = valid_of(r)
            s = jnp.where(valid, s_all[rows(r)] - slopes[r] * d_f, NEG_INF)
            m_prev = m_ref[rows(r), :]
            m_new = jnp.maximum(m_prev, jnp.max(s, axis=-1, keepdims=True))
            alpha = jnp.exp(m_prev - m_new)
            e = jnp.where(valid, jnp.exp(s - m_new), 0.0)
            l_ref[rows(r), :] = alpha * l_ref[rows(r), :] + jnp.sum(e, axis=-1, keepdims=True)
            m_ref[rows(r), :] = m_new
            acc_ref[rows(r), :] = alpha * acc_ref[rows(r), :]
            es.append(e)
        acc_ref[...] += _dot_nt(jnp.concatenate(es, axis=0).astype(BF16), v_t.astype(BF16))

    def finish(branch):
        for r in range(NSA_HEADS):
            l = l_ref[rows(r), :]
            inv = jnp.where(l > 0.0, 1.0 / jnp.where(l > 0.0, l, 1.0), 0.0)
            out_ref[rows(r), :] += gate(r, branch) * (acc_ref[rows(r), :] * inv)

    eb_m = lax.broadcasted_iota(jnp.int32, (nbp, KEY_BLOCK), 0)
    eb_l = lax.broadcasted_iota(jnp.int32, (nbp, KEY_BLOCK), 1)
    eb_off = jnp.where(eb_l >= SEL_BLOCK, 1, 0)

    def sel_body(kb, carry):
        k0 = pl.multiple_of(kb * KEY_BLOCK, KEY_BLOCK)
        expand = jnp.where(eb_m == 2 * kb + eb_off, 1.0, 0.0).astype(BF16)
        member = _dot(sel, expand)
        d = t_col - (kb * KEY_BLOCK + lane)
        causal = d >= 0
        attend(ks_ref[0, :, pl.ds(k0, KEY_BLOCK)], vs_ref[0, :, pl.ds(k0, KEY_BLOCK)], d.astype(F32),
               lambda r: causal & (member[(r // 4) * tqs:(r // 4) * tqs + tq] > 0.5))
        return carry

    reset()
    lax.fori_loop(0, (t0 + tq + KEY_BLOCK - 1) // KEY_BLOCK, sel_body, 0)
    finish(1)

    def win_body(wb, carry):
        k0 = pl.multiple_of(wb * KEY_BLOCK, KEY_BLOCK)
        d = t_col - (win_start + wb * KEY_BLOCK + lane)
        valid = (d >= 0) & (d < WINDOW)
        attend(kw_ref[0, :, pl.ds(k0, KEY_BLOCK)], vw_ref[0, :, pl.ds(k0, KEY_BLOCK)], d.astype(F32),
               lambda r: valid)
        return carry

    reset()
    w_first = jnp.maximum(t0 - (WINDOW - 1) - win_start, 0) // KEY_BLOCK
    w_last = (t0 + tq - 1 - win_start) // KEY_BLOCK
    lax.fori_loop(w_first, w_last + 1, win_body, 0)
    finish(2)

    for ti in range(4):
        o_ref[0, :, ti * LANES:(ti + 1) * LANES] = jnp.where(lo, out_ref[rows(ti), :], out_ref[rows(4 + ti), :])


def _nsa_attention(q, gates, cmp, sel_kv, win_kv, *, q_start, win_start, n_keys):
    batch, tokens, _ = q.shape
    tq = min(tokens, KEY_BLOCK)
    ncp = cmp.shape[1]
    n_cmp = (n_keys - CMP_LEN) // CMP_STRIDE + 1
    n_blk = -(-n_keys // SEL_BLOCK)
    nbp = -(-n_blk // LANES) * LANES
    tqs = max(tq, LANES)
    tkp, twp = sel_kv.shape[2], win_kv.shape[2]
    return pl.pallas_call(
        functools.partial(_nsa_kernel, tq=tq, q_start=q_start, win_start=win_start, n_cmp=n_cmp, n_blk=n_blk),
        grid=(batch, tokens // tq),
        in_specs=[pl.BlockSpec((1, tq, 512), lambda b, i: (b, i, 0)),
                  pl.BlockSpec((1, tq, LANES), lambda b, i: (b, i, 0)),
                  pl.BlockSpec((1, ncp, 256), lambda b, i: (b, 0, 0)),
                  pl.BlockSpec((1, LANES, tkp), lambda b, i: (b, 0, 0)),
                  pl.BlockSpec((1, LANES, tkp), lambda b, i: (b, 1, 0)),
                  pl.BlockSpec((1, LANES, twp), lambda b, i: (b, 0, 0)),
                  pl.BlockSpec((1, LANES, twp), lambda b, i: (b, 1, 0))],
        out_specs=pl.BlockSpec((1, tq, 512), lambda b, i: (b, i, 0)),
        out_shape=jax.ShapeDtypeStruct((batch, tokens, 512), F32),
        scratch_shapes=[pltpu.VMEM((NSA_HEADS * tq, LANES), F32),
                        pltpu.VMEM((NSA_HEADS * tq, LANES), F32),
                        pltpu.VMEM((NSA_HEADS * tq, LANES), F32),
                        pltpu.VMEM((NSA_HEADS * tq, LANES), F32),
                        pltpu.VMEM((nbp, 2 * tqs), F32)],
        compiler_params=_params(("parallel", "arbitrary")),
        name="nsa_attention",
    )(q, gates, cmp, sel_kv, sel_kv, win_kv, win_kv)


def _sb_kernel(q_ref, kd_ref, vd_ref, kp_ref, vp_ref, o_ref, acc_ref, r_ref, *, tq, past_blocks):
    i = pl.program_id(2)
    lane = lax.broadcasted_iota(jnp.int32, (tq, LANES), 1)
    lo = lane < HEAD_DIM
    q = q_ref[0] * ATTN_SCALE
    qs = jnp.concatenate([jnp.where(lo, q, 0.0), jnp.where(lo, 0.0, q)], axis=0).astype(BF16)
    u_r = lax.broadcasted_iota(jnp.int32, (KEY_BLOCK, 2 * KEY_BLOCK), 0)
    u_c = lax.broadcasted_iota(jnp.int32, (KEY_BLOCK, 2 * KEY_BLOCK), 1)
    later_and_total = jnp.where((u_c >= KEY_BLOCK) | (u_r > u_c), 1.0, 0.0).astype(BF16)
    q_idx = lax.broadcasted_iota(jnp.int32, (2 * tq, KEY_BLOCK), 0)
    q_idx = jnp.where(q_idx >= tq, q_idx - tq, q_idx)
    causal = lax.broadcasted_iota(jnp.int32, (2 * tq, KEY_BLOCK), 1) < q_idx

    def step(k_t, v_t, mask):
        z = _dot(qs, k_t.astype(BF16))
        sp = jnp.maximum(z, 0.0) + jnp.log(1.0 + jnp.exp(-jnp.abs(z)))
        if mask is not None:
            sp = jnp.where(mask, sp, 0.0)
        hi, lo_part = _split_bf16(sp)
        sums = _dot(hi, later_and_total) + _dot(lo_part, later_and_total)
        a = jnp.exp(z - sp - (sums[:, :KEY_BLOCK] + r_ref[...]))
        if mask is not None:
            a = jnp.where(mask, a, 0.0)
        acc_ref[...] += _dot_nt(a.astype(BF16), v_t.astype(BF16))
        r_ref[...] += sums[:, KEY_BLOCK:]

    acc_ref[...] = jnp.zeros_like(acc_ref)
    r_ref[...] = jnp.zeros_like(r_ref)
    step(kd_ref[0], vd_ref[0], causal)

    n_past = past_blocks + i

    def past_body(s, carry):
        k0 = pl.multiple_of((n_past - 1 - s) * KEY_BLOCK, KEY_BLOCK)
        step(kp_ref[0, :, pl.ds(k0, KEY_BLOCK)], vp_ref[0, :, pl.ds(k0, KEY_BLOCK)], None)
        return carry

    lax.fori_loop(0, n_past, past_body, 0)
    o_ref[0] = jnp.where(lo, acc_ref[0:tq, :], acc_ref[tq:2 * tq, :])


def _sb_attention(q, kv, *, past_blocks):
    batch, tokens, _ = q.shape
    tq = min(tokens, KEY_BLOCK)
    tkp = kv.shape[2]
    pairs = SB_HEADS // 2
    return pl.pallas_call(
        functools.partial(_sb_kernel, tq=tq, past_blocks=past_blocks),
        grid=(batch, pairs, tokens // tq),
        in_specs=[pl.BlockSpec((1, tq, LANES), lambda b, p, i: (b, i, p)),
                  pl.BlockSpec((1, LANES, KEY_BLOCK), lambda b, p, i: (b, p, past_blocks + i)),
                  pl.BlockSpec((1, LANES, KEY_BLOCK), lambda b, p, i: (b, pairs + p, past_blocks + i)),
                  pl.BlockSpec((1, LANES, tkp), lambda b, p, i: (b, p, 0)),
                  pl.BlockSpec((1, LANES, tkp), lambda b, p, i: (b, pairs + p, 0))],
        out_specs=pl.BlockSpec((1, tq, LANES), lambda b, p, i: (b, i, p)),
        out_shape=jax.ShapeDtypeStruct((batch, tokens, 512), F32),
        scratch_shapes=[pltpu.VMEM((2 * tq, LANES), F32), pltpu.VMEM((2 * tq, LANES), F32)],
        compiler_params=_params(("parallel", "parallel", "arbitrary")),
        name="stick_breaking_attention",
    )(q, kv, kv, kv, kv)


def _prep_weights(ffn1_w_in, ffn1_w_out, ffn2_w_in, ffn2_w_out, w_in, nsa_q_norm, nsa_k_norm,
                  cmp_pe, cmp_w1, cmp_w2, w_up_nsa, w_up_sb, w_out):
    w_perm = jnp.concatenate([w_in[:, _QPERM], w_in[:, 1304:1816], w_in[:, 2840:4888], w_in[:, 1280:1304],
                              jnp.zeros((D_MODEL, _C_END - _C_GN - 24), F32)], axis=1).astype(BF16)
    w_kv_t = jnp.concatenate([w_in[:, 512:1280], w_in[:, 1816:2840]], axis=1).T.astype(BF16)
    tile2 = lambda g: jnp.tile(g, 2).reshape(1, LANES)
    eye2 = jnp.eye(NSA_KV, dtype=F32)

    def big_w1(kv, half):
        w = cmp_w1[kv, half * CMP_STRIDE:(half + 1) * CMP_STRIDE]
        full = jnp.einsum('lde,gf->lgdfe', w, eye2)
        return full.reshape(CMP_STRIDE * LANES, LANES).astype(BF16)

    def pe_row(kv, half):
        pe = cmp_pe[kv, half * CMP_STRIDE:(half + 1) * CMP_STRIDE]
        return jnp.tile(pe[:, None, :], (1, NSA_KV, 1)).reshape(1, CMP_STRIDE * LANES)

    order = [(kv, half) for kv in range(2) for half in range(2)]
    w2_big = jnp.stack([jnp.einsum('de,gf->gdfe', cmp_w2[kv], eye2).reshape(LANES, LANES) for kv in range(2)])
    return dict(
        f1_in=ffn1_w_in.astype(BF16), f1_out=ffn1_w_out.astype(BF16),
        f2_in=ffn2_w_in.astype(BF16), f2_out=ffn2_w_out.astype(BF16),
        w_perm=w_perm, w_kv_t=w_kv_t, q_gain=tile2(nsa_q_norm),
        k_gain_cmp=tile2(nsa_k_norm[0]),
        k_gain2=jnp.stack([jnp.tile(nsa_k_norm[1], 2), jnp.tile(nsa_k_norm[2], 2)]).reshape(2, LANES, 1),
        pe4=jnp.concatenate([pe_row(kv, half) for kv, half in order], axis=0),
        w1_big=jnp.stack([big_w1(kv, half) for kv, half in order]), w2_big=w2_big.astype(BF16),
        w_up_nsa=w_up_nsa[_QPERM].astype(BF16), w_up_sb=w_up_sb.astype(BF16), w_out=w_out.astype(BF16))


def _pad_tokens(x, n):
    return jnp.pad(x, ((0, 0), (0, 0), (0, n - x.shape[2])))


def _per_sequence(x_t, batch, tokens):
    if x_t.shape[0] == batch:
        return x_t
    return x_t.reshape(x_t.shape[1], batch, tokens).transpose(1, 0, 2)


def _layer(x, mod, past, norms, w):
    batch, tokens, _ = x.shape
    n1, nm, n2 = norms
    x = _ffn(x, mod, 0, n1, w['f1_in'], w['f1_out'])
    q, sq, g_mn, g_ms, g_nsa, new_cmp, new_sel, win_rows, new_sb = _projection(
        x, mod, nm, w['w_perm'], w['w_kv_t'], w['q_gain'], w['k_gain2'])
    new_cmp, new_sel, win_rows, new_sb = (_per_sequence(a, batch, tokens) for a in (new_cmp, new_sel, win_rows, new_sb))
    if past is None:
        q_start, win_start = 0, 0
        cmp_rows, sel_rows, sb_rows, win_all = new_cmp, new_sel, new_sb, win_rows
        new_win = win_rows[:, :, tokens - min(WINDOW, tokens):]
    else:
        cmp_rows, sel_rows, sb_rows, win_buf = past(new_sel, new_sb)
        q_start = cmp_rows.shape[2]
        win_cat = jnp.concatenate([win_buf, win_rows], axis=2)
        win_start = q_start - win_buf.shape[2]
        new_win = win_cat[:, :, win_cat.shape[2] - min(WINDOW, win_cat.shape[2]):]
        win_all = _pad_tokens(win_cat, -(-win_cat.shape[2] // KEY_BLOCK) * KEY_BLOCK)
    assert q_start % KEY_BLOCK == 0
    n_keys = q_start + tokens
    n_half = (n_keys - CMP_LEN) // CMP_STRIDE + 2
    assert n_half * CMP_STRIDE == cmp_rows.shape[2] and cmp_rows.shape[2] % LANES == 0
    cmp = _compress(cmp_rows, w['pe4'], w['w1_big'], w['w2_big'], w['k_gain_cmp'])
    o_nsa = _nsa_attention(q, g_nsa, cmp, sel_rows, win_all, q_start=q_start, win_start=win_start, n_keys=n_keys)
    o_sb = _sb_attention(sq, sb_rows, past_blocks=q_start // KEY_BLOCK)
    x = _merge(x, mod, o_nsa, o_sb, g_mn, g_ms, w['w_up_nsa'], w['w_up_sb'], w['w_out'])
    x = _ffn(x, mod, 6, n2, w['f2_in'], w['f2_out'])
    return x, new_cmp, new_sel, new_sb, new_win


def kernel(x_prompt, x_sample, c_prompt, c_sample, cache_nsa_cmp, cache_nsa_sel, cache_sb, state_nsa_win, page_table, norm_ffn1, norm_mix, norm_ffn2, w_ada, b_ada, ffn1_w_in, ffn1_w_out, ffn2_w_in, ffn2_w_out, w_in, nsa_q_norm, nsa_k_norm, cmp_pe, cmp_w1, cmp_w2, w_up_nsa, w_up_sb, w_out):
    depth = w_in.shape[0]
    assert depth == 1, "single-layer trunk"
    bp, tp, _ = x_prompt.shape
    bs, ts, _ = x_sample.shape
    n_pool = cache_nsa_cmp.shape[1]
    w = _prep_weights(ffn1_w_in[0], ffn1_w_out[0], ffn2_w_in[0], ffn2_w_out[0], w_in[0], nsa_q_norm[0],
                      nsa_k_norm[0], cmp_pe[0], cmp_w1[0], cmp_w2[0], w_up_nsa[0], w_up_sb[0], w_out[0])
    norms = (norm_ffn1[0], norm_mix[0], norm_ffn2[0])
    mod = _modulation(jnp.concatenate([c_prompt, c_sample], axis=0), w_ada[0], b_ada[0])
    mod_p = mod[:bp].reshape(bp, 1, 9 * D_MODEL)
    mod_s = mod[bp:].reshape(bs, 1, 9 * D_MODEL)

    yp, cmp_p, sel_p, sb_p, win_p = _layer(x_prompt, mod_p, None, norms, w)

    def pages_t(cache):
        return cache[0].transpose(0, 2, 3, 4, 1).reshape(n_pool, -1, PAGE)

    def past(new_sel, new_sb):
        cmp_rows = _gather_pages(pages_t(cache_nsa_cmp), page_table, None)
        sel_rows = _gather_pages(pages_t(cache_nsa_sel), page_table, _pad_tokens(new_sel, PAGE))
        sb_rows = _gather_pages(pages_t(cache_sb), page_table, _pad_tokens(new_sb, PAGE))
        win_buf = state_nsa_win[0].transpose(0, 2, 3, 4, 1).reshape(bs, 256, state_nsa_win.shape[2])
        return cmp_rows, sel_rows, sb_rows, win_buf

    ys, cmp_s, sel_s, sb_s, win_s = _layer(x_sample, mod_s, past, norms, w)

    def rows(a, heads):
        return a.reshape(1, a.shape[0], 2, heads, HEAD_DIM, a.shape[2]).transpose(0, 1, 5, 2, 3, 4)

    return (yp, ys, rows(cmp_p, NSA_KV), rows(sel_p, NSA_KV), rows(sb_p, SB_HEADS), rows(win_p, NSA_KV),
            rows(cmp_s, NSA_KV), rows(sel_s, NSA_KV), rows(sb_s, SB_HEADS), rows(win_s, NSA_KV))
```

```python
import functools

import numpy as np
import jax
import jax.numpy as jnp
from jax import lax
from jax.experimental import pallas as pl
from jax.experimental.pallas import tpu as pltpu

F32 = jnp.float32
BF16 = jnp.bfloat16

D_MODEL = 1024
HEAD_DIM = 64
NSA_HEADS = 8
NSA_KV = 2
SB_HEADS = 8
CMP_LEN = 32
CMP_STRIDE = 16
SEL_BLOCK = 64
SEL_TOPK = 16
WINDOW = 512
D_FF = 2816
EPS = 1e-6
NEG_INF = -1.0e30
FORCE_SCORE = 1.0e6
PAGE = 128

LANES = 128
KEY_BLOCK = 128
VMEM_LIMIT = 56 * 1024 * 1024
ATTN_SCALE = HEAD_DIM ** -0.5

_QPERM = np.concatenate([np.concatenate([np.arange(64 * i, 64 * i + 64), np.arange(64 * (4 + i), 64 * (4 + i) + 64)])
                         for i in range(4)])
_C_Q, _C_SQ, _C_GMN, _C_GMS, _C_GN, _C_END = 0, 512, 1024, 2048, 3072, 3200
_R_CMP, _R_SEL, _R_WIN, _R_SB, _R_END = 0, 256, 512, 768, 1792


def _params(sem, vmem=VMEM_LIMIT):
    return pltpu.CompilerParams(dimension_semantics=sem, vmem_limit_bytes=vmem)


def _dot(a, b):
    return jnp.dot(a, b, preferred_element_type=F32)


def _dot_nt(a, b):
    return lax.dot_general(a, b, (((1,), (1,)), ((), ())), preferred_element_type=F32)


def _split_bf16(x):
    hi = x.astype(BF16)
    lo = (x - hi.astype(F32)).astype(BF16)
    return hi, lo


def _silu(x):
    return x * jax.nn.sigmoid(x)


def _norm_pairs(x, gain):
    outs = []
    for c in range(x.shape[1] // LANES):
        xt = x[:, c * LANES:(c + 1) * LANES]
        lo = lax.broadcasted_iota(jnp.int32, xt.shape, 1) < HEAD_DIM
        sq = xt * xt
        ms_lo = jnp.sum(jnp.where(lo, sq, 0.0), axis=-1, keepdims=True) * (1.0 / HEAD_DIM)
        ms_hi = jnp.sum(jnp.where(lo, 0.0, sq), axis=-1, keepdims=True) * (1.0 / HEAD_DIM)
        inv = jnp.where(lo, lax.rsqrt(ms_lo + EPS), lax.rsqrt(ms_hi + EPS))
        outs.append(xt * inv * gain)
    return outs[0] if len(outs) == 1 else jnp.concatenate(outs, axis=1)


def _ada_norm(x, gain, shift, scale):
    ms = jnp.mean(x * x, axis=-1, keepdims=True)
    n = x * lax.rsqrt(ms + EPS) * gain
    h = n * (1.0 + scale) + shift
    return h.reshape(x.shape[0] * x.shape[1], x.shape[2]).astype(BF16)


def _mod_kernel(c_ref, w_ref, b_ref, o_ref):
    s = _silu(c_ref[...]).astype(BF16)
    o_ref[...] = _dot(s, w_ref[...].astype(BF16)) + b_ref[...]


def _modulation(c, w_ada, b_ada):
    rows = c.shape[0]
    n_out = w_ada.shape[1]
    tn = D_MODEL
    return pl.pallas_call(
        _mod_kernel,
        grid=(n_out // tn,),
        in_specs=[pl.BlockSpec((rows, D_MODEL), lambda j: (0, 0)),
                  pl.BlockSpec((D_MODEL, tn), lambda j: (0, j)),
                  pl.BlockSpec((1, tn), lambda j: (0, j))],
        out_specs=pl.BlockSpec((rows, tn), lambda j: (0, j)),
        out_shape=jax.ShapeDtypeStruct((rows, n_out), F32),
        compiler_params=_params(("arbitrary",)),
        name="adaln_modulation",
    )(c, w_ada, b_ada.reshape(1, n_out))


def _ffn_kernel(x_ref, sh_ref, sc_ref, gt_ref, g_ref, wi_ref, wo_ref, o_ref, *, chunk):
    x = x_ref[...]
    hb = _ada_norm(x, g_ref[...], sh_ref[...], sc_ref[...])
    acc = None
    for c0 in range(0, D_FF, chunk):
        a = _dot(hb, wi_ref[:, c0:c0 + chunk])
        b = _dot(hb, wi_ref[:, D_FF + c0:D_FF + c0 + chunk])
        act = (_silu(a) * b).astype(BF16)
        part = _dot(act, wo_ref[c0:c0 + chunk, :])
        acc = part if acc is None else acc + part
    o_ref[...] = x + 0.5 * gt_ref[...] * acc.reshape(x.shape)


def _row_tiles(batch, tokens, rows=512):
    if tokens >= rows:
        return 1, rows
    bb = max(1, min(batch, rows // tokens))
    while batch % bb:
        bb -= 1
    return bb, tokens


def _mod_spec(bb, k):
    return pl.BlockSpec((bb, 1, D_MODEL), lambda i, j, k=k: (i, 0, k))


def _const_spec(shape):
    nd = len(shape)
    return pl.BlockSpec(shape, lambda i, j: (0,) * nd, pipeline_mode=pl.Buffered(1))


def _ffn(x, mod, k0, gain, w_in, w_out):
    batch, tokens, _ = x.shape
    bb, tt = _row_tiles(batch, tokens)
    xspec = pl.BlockSpec((bb, tt, D_MODEL), lambda i, j: (i, j, 0))
    return pl.pallas_call(
        functools.partial(_ffn_kernel, chunk=D_FF // 2),
        grid=(batch // bb, tokens // tt),
        in_specs=[xspec, _mod_spec(bb, k0), _mod_spec(bb, k0 + 1), _mod_spec(bb, k0 + 2),
                  _const_spec((1, D_MODEL)), _const_spec((D_MODEL, 2 * D_FF)), _const_spec((D_FF, D_MODEL))],
        out_specs=xspec,
        out_shape=jax.ShapeDtypeStruct(x.shape, F32),
        compiler_params=_params(("parallel", "parallel")),
        name="ffn_swiglu",
    )(x, mod, mod, mod, gain.reshape(1, D_MODEL), w_in, w_out)


def _norm_groups_t(x, gain_col):
    outs = []
    for g in range(NSA_KV):
        xg = x[g * HEAD_DIM:(g + 1) * HEAD_DIM]
        ms = jnp.mean(xg * xg, axis=0, keepdims=True)
        outs.append(xg * lax.rsqrt(ms + EPS) * gain_col[g * HEAD_DIM:(g + 1) * HEAD_DIM])
    return jnp.concatenate(outs, axis=0)


def _proj_kernel(x_ref, sh_ref, sc_ref, g_ref, w_ref, wt_ref, qg_ref, kg_ref,
                 q_o, sq_o, gmn_o, gms_o, gn_o, cmp_o, sel_o, win_o, sb_o):
    x = x_ref[...]
    bb, tt, _ = x.shape
    hb = _ada_norm(x, g_ref[...], sh_ref[...], sc_ref[...])

    def cols(c0, c1):
        return _dot(hb, w_ref[:, c0:c1])

    def rows_t(r0, r1):
        return _dot_nt(wt_ref[r0:r1, :], hb)

    def put(ref, val):
        ref[...] = val.reshape(bb, tt, val.shape[-1])

    put(q_o, _norm_pairs(cols(_C_Q, _C_SQ), qg_ref[...]))
    put(sq_o, cols(_C_SQ, _C_GMN))
    put(gmn_o, jax.nn.sigmoid(cols(_C_GMN, _C_GMS)))
    put(gms_o, jax.nn.sigmoid(cols(_C_GMS, _C_GN)))
    put(gn_o, jax.nn.sigmoid(cols(_C_GN, _C_END)))
    cmp_o[0] = rows_t(_R_CMP, _R_SEL)
    sel_o[0, 0:LANES, :] = _norm_groups_t(rows_t(_R_SEL, _R_SEL + LANES), kg_ref[0])
    sel_o[0, LANES:2 * LANES, :] = rows_t(_R_SEL + LANES, _R_WIN)
    win_o[0, 0:LANES, :] = _norm_groups_t(rows_t(_R_WIN, _R_WIN + LANES), kg_ref[1])
    win_o[0, LANES:2 * LANES, :] = rows_t(_R_WIN + LANES, _R_SB)
    sb_o[0] = rows_t(_R_SB, _R_END)


def _projection(x, mod, gain, w_perm, w_kv_t, q_gain, k_gain2):
    batch, tokens, _ = x.shape
    bb, tt = _row_tiles(batch, tokens)
    widths = (512, 512, 1024, 1024, 128)
    feats = (256, 256, 256, 1024)

    def rspec(w):
        return pl.BlockSpec((bb, tt, w), lambda i, j: (i, j, 0))

    if bb == 1:
        t_shape = lambda f: (batch, f, tokens)
        tspec = lambda f: pl.BlockSpec((1, f, tt), lambda i, j: (i, 0, j))
    else:
        assert tt == tokens
        t_shape = lambda f: (1, f, batch * tokens)
        tspec = lambda f: pl.BlockSpec((1, f, bb * tt), lambda i, j: (0, 0, i))

    return pl.pallas_call(
        _proj_kernel,
        grid=(batch // bb, tokens // tt),
        in_specs=[rspec(D_MODEL), _mod_spec(bb, 3), _mod_spec(bb, 4),
                  _const_spec((1, D_MODEL)), _const_spec((D_MODEL, _C_END)), _const_spec((_R_END, D_MODEL)),
                  _const_spec((1, LANES)), _const_spec((2, LANES, 1))],
        out_specs=[rspec(w) for w in widths] + [tspec(f) for f in feats],
        out_shape=([jax.ShapeDtypeStruct((batch, tokens, w), F32) for w in widths]
                   + [jax.ShapeDtypeStruct(t_shape(f), F32) for f in feats]),
        compiler_params=_params(("parallel", "parallel")),
        name="mixer_in_proj",
    )(x, mod, mod, gain.reshape(1, D_MODEL), w_perm, w_kv_t, q_gain, k_gain2)


def _merge_kernel(x_ref, gt_ref, on_ref, os_ref, gmn_ref, gms_ref, wun_ref, wus_ref, wo_ref, o_ref):
    x = x_ref[...]
    rows = x.shape[0] * x.shape[1]

    def flat(ref):
        v = ref[...]
        return v.reshape(rows, v.shape[-1])

    y = (flat(gmn_ref) * _dot(flat(on_ref).astype(BF16), wun_ref[...])
         + flat(gms_ref) * _dot(flat(os_ref).astype(BF16), wus_ref[...]))
    m = _dot(y.astype(BF16), wo_ref[...])
    o_ref[...] = x + gt_ref[...] * m.reshape(x.shape)


def _merge(x, mod, o_nsa, o_sb, g_mn, g_ms, w_up_nsa, w_up_sb, w_out):
    batch, tokens, _ = x.shape
    bb, tt = _row_tiles(batch, tokens)

    def rspec(w):
        return pl.BlockSpec((bb, tt, w), lambda i, j: (i, j, 0))

    return pl.pallas_call(
        _merge_kernel,
        grid=(batch // bb, tokens // tt),
        in_specs=[rspec(D_MODEL), _mod_spec(bb, 5), rspec(512), rspec(512), rspec(D_MODEL), rspec(D_MODEL),
                  _const_spec((512, D_MODEL)), _const_spec((512, D_MODEL)), _const_spec((D_MODEL, D_MODEL))],
        out_specs=rspec(D_MODEL),
        out_shape=jax.ShapeDtypeStruct(x.shape, F32),
        compiler_params=_params(("parallel", "parallel")),
        name="mixer_out_merge",
    )(x, mod, o_nsa, o_sb, g_mn, g_ms, w_up_nsa, w_up_sb, w_out)


def _stream_pages(pt_ref, cache_ref, buf_ref, sem_ref, step, n_steps, pages_of):
    n = buf_ref.shape[1]

    def copies(at_step, slot):
        seq, page0 = pages_of(at_step)

        def copy(j):
            return pltpu.make_async_copy(cache_ref.at[pt_ref[seq, page0 + j]], buf_ref.at[slot, j], sem_ref.at[slot])
        return copy

    def start(at_step, slot):
        copy = copies(at_step, slot)
        lax.fori_loop(0, n, lambda j, c: (copy(j).start(), c)[1], 0)

    slot = step % 2

    @pl.when(step == 0)
    def _():
        start(step, slot)

    @pl.when(step + 1 < n_steps)
    def _():
        start(step + 1, 1 - slot)

    copy = copies(step, slot)
    lax.fori_loop(0, n, lambda j, c: (copy(j).wait(), c)[1], 0)
    return slot


def _paged_call(body, grid, page_table, cache, operands, in_specs, out_specs, out_shape, pages_per_step,
                scratch_shapes, name):
    return pl.pallas_call(
        body,
        grid_spec=pltpu.PrefetchScalarGridSpec(
            num_scalar_prefetch=1,
            grid=grid,
            in_specs=[pl.BlockSpec(memory_space=pl.ANY)] + in_specs,
            out_specs=out_specs,
            scratch_shapes=[pltpu.VMEM((2, pages_per_step) + cache.shape[1:], F32),
                            pltpu.SemaphoreType.DMA((2,))] + scratch_shapes),
        out_shape=out_shape,
        compiler_params=_params(("arbitrary",) * len(grid)),
        name=name,
    )(page_table, cache, *operands)


def _cmp_core(load_chunk, pe_ref, w1_ref, w2_ref, kg_ref, o_ref, tok_ref):
    n_tok = tok_ref.shape[1]
    rows = n_tok // CMP_STRIDE

    def to_token_major(c, carry):
        c0 = pl.multiple_of(c * LANES, LANES)
        xt = load_chunk(c).T
        tok_ref[0, pl.ds(c0, LANES), :] = xt[:, 0:LANES]
        tok_ref[1, pl.ds(c0, LANES), :] = xt[:, LANES:2 * LANES]
        return carry

    lax.fori_loop(0, n_tok // LANES, to_token_major, 0)
    for kv in range(2):
        x = jnp.concatenate([tok_ref[kv, pl.ds(l, rows, stride=CMP_STRIDE), :] for l in range(CMP_STRIDE)], axis=1)
        a = _dot((x + pe_ref[2 * kv:2 * kv + 1, :]).astype(BF16), w1_ref[2 * kv])
        b = _dot((x + pe_ref[2 * kv + 1:2 * kv + 2, :]).astype(BF16), w1_ref[2 * kv + 1])
        h1 = a + pltpu.roll(b, rows - 1, axis=0)
        c = _dot(_silu(h1).astype(BF16), w2_ref[kv])
        o_ref[0, :, kv * LANES:(kv + 1) * LANES] = _norm_pairs(c, kg_ref[...]) if kv == 0 else c


def _cmp_kernel(x_ref, pe_ref, w1_ref, w2_ref, kg_ref, o_ref, tok_ref):
    _cmp_core(lambda c: x_ref[0, :, pl.ds(pl.multiple_of(c * LANES, LANES), LANES)],
              pe_ref, w1_ref, w2_ref, kg_ref, o_ref, tok_ref)


def _cmp_paged_kernel(pt_ref, cache_ref, pe_ref, w1_ref, w2_ref, kg_ref, o_ref, buf_ref, sem_ref, tok_ref):
    slot = _stream_pages(pt_ref, cache_ref, buf_ref, sem_ref, pl.program_id(0), pl.num_programs(0),
                         lambda step: (step, 0))
    _cmp_core(lambda c: buf_ref[slot, c], pe_ref, w1_ref, w2_ref, kg_ref, o_ref, tok_ref)


def _cmp_specs(rows):
    width = CMP_STRIDE * LANES
    in_specs = [pl.BlockSpec((4, width), lambda b, *_: (0, 0)),
                pl.BlockSpec((4, width, LANES), lambda b, *_: (0, 0, 0)),
                pl.BlockSpec((2, LANES, LANES), lambda b, *_: (0, 0, 0)),
                pl.BlockSpec((1, LANES), lambda b, *_: (0, 0))]
    return in_specs, pl.BlockSpec((1, rows, 256), lambda b, *_: (b, 0, 0))


def _compress(x_t, pe4, w1_big, w2_big, k_gain):
    batch, feats, n_tok = x_t.shape
    rows = n_tok // CMP_STRIDE
    in_specs, out_spec = _cmp_specs(rows)
    return pl.pallas_call(
        _cmp_kernel,
        grid=(batch,),
        in_specs=[pl.BlockSpec((1, feats, n_tok), lambda b: (b, 0, 0))] + in_specs,
        out_specs=out_spec,
        out_shape=jax.ShapeDtypeStruct((batch, rows, 256), F32),
        scratch_shapes=[pltpu.VMEM((2, n_tok, LANES), F32)],
        compiler_params=_params(("parallel",)),
        name="nsa_compress",
    )(x_t, pe4, w1_big, w2_big, k_gain)


def _compress_paged(page_table, cache, pe4, w1_big, w2_big, k_gain):
    batch, n_pages = page_table.shape
    n_tok = n_pages * PAGE
    rows = n_tok // CMP_STRIDE
    in_specs, out_spec = _cmp_specs(rows)
    return _paged_call(_cmp_paged_kernel, (batch,), page_table, cache, (pe4, w1_big, w2_big, k_gain), in_specs,
                       out_spec, jax.ShapeDtypeStruct((batch, rows, 256), F32), n_pages,
                       [pltpu.VMEM((2, n_tok, LANES), F32)], "nsa_compress_paged")


def _nsa_core(q_ref, g_ref, c_ref, kw_ref, vw_ref, o_ref, acc_ref, m_ref, l_ref, out_ref, imp_ref,
              *, i, sel_block, n_sel_blocks, sel_tail, tq, q_start, win_start, n_cmp, n_blk):
    ncp = c_ref.shape[1]
    nbp = imp_ref.shape[0]
    tqs = imp_ref.shape[1] // 2
    t0 = q_start + i * tq
    slopes = [2.0 ** (-(h + 1)) for h in range(NSA_HEADS)]

    lane = lax.broadcasted_iota(jnp.int32, (tq, LANES), 1)
    lo = lane < HEAD_DIM
    t_col = t0 + lax.broadcasted_iota(jnp.int32, (tq, LANES), 0)

    tiles = [q_ref[0, :, ti * LANES:(ti + 1) * LANES] * ATTN_SCALE for ti in range(4)]
    qs = jnp.concatenate([jnp.where(lo, t, 0.0) for t in tiles] + [jnp.where(lo, 0.0, t) for t in tiles],
                         axis=0).astype(BF16)
    gates = g_ref[0]

    def gate(r, branch):
        return gates[:, 3 * r + branch:3 * r + branch + 1]

    def rows(r):
        return slice(r * tq, (r + 1) * tq)

    kc = c_ref[0, :, 0:LANES].astype(BF16)
    vc = c_ref[0, :, LANES:2 * LANES].astype(BF16)
    s_c = _dot_nt(qs, kc)
    n_idx = lax.broadcasted_iota(jnp.int32, (tq, ncp), 1)
    d_c = (t0 + lax.broadcasted_iota(jnp.int32, (tq, ncp), 0)) - (CMP_STRIDE * n_idx + (CMP_LEN - 1))
    valid_c = (d_c >= 0) & (n_idx < n_cmp)
    d_cf = d_c.astype(F32)
    probs = []
    for r in range(NSA_HEADS):
        s = jnp.where(valid_c, s_c[rows(r)] - slopes[r] * d_cf, NEG_INF)
        m = jnp.max(s, axis=-1, keepdims=True)
        e = jnp.where(valid_c, jnp.exp(s - m), 0.0)
        l = jnp.sum(e, axis=-1, keepdims=True)
        probs.append(e / jnp.where(l > 0.0, l, 1.0))
    o_c = _dot(jnp.concatenate(probs, axis=0).astype(BF16), vc)
    for r in range(NSA_HEADS):
        out_ref[rows(r), :] = gate(r, 0) * o_c[rows(r)]

    psum = []
    for g in range(NSA_KV):
        ps = probs[4 * g] + probs[4 * g + 1] + probs[4 * g + 2] + probs[4 * g + 3]
        if tqs > tq:
            ps = jnp.concatenate([ps, jnp.zeros((tqs - tq, ncp), F32)], axis=0)
        psum.append(ps)
    p_hi, p_lo = _split_bf16(jnp.concatenate(psum, axis=0))
    m_o = lax.broadcasted_iota(jnp.int32, (nbp, ncp), 0)
    n_o = lax.broadcasted_iota(jnp.int32, (nbp, ncp), 1)
    ov_t = jnp.where((n_o >= 4 * m_o - 1) & (n_o <= 4 * m_o + 3) & (n_o < n_cmp), 1.0, 0.0).astype(BF16)
    imp = _dot_nt(ov_t, p_hi) + _dot_nt(ov_t, p_lo)
    m_i = lax.broadcasted_iota(jnp.int32, (nbp, 2 * tqs), 0)
    q_i = lax.broadcasted_iota(jnp.int32, (nbp, 2 * tqs), 1)
    t_row = t0 + jnp.where(q_i >= tqs, q_i - tqs, q_i)
    blk_t = t_row >> 6
    forced = (m_i == 0) | (m_i == blk_t) | (m_i == blk_t - 1)
    future = (m_i * SEL_BLOCK > t_row) | (m_i >= n_blk)
    imp = jnp.where(future, -1.0, jnp.where(forced, FORCE_SCORE, imp))
    imp_ref[...] = imp

    def rank_body(j, cnt):
        row = imp_ref[pl.ds(j, 1), :]
        tie = jnp.where(m_i > j, 1.0, 0.0)
        return cnt + jnp.where(row > imp, 1.0, jnp.where(row == imp, tie, 0.0))

    cnt = lax.fori_loop(0, n_blk, rank_body, jnp.zeros((nbp, 2 * tqs), F32))
    sel_t = jnp.where(cnt < float(SEL_TOPK), jnp.where(imp >= 0.0, 1.0, 0.0), 0.0).astype(BF16)
    e_r = lax.broadcasted_iota(jnp.int32, (2 * tqs, 2 * tqs), 0)
    e_c = lax.broadcasted_iota(jnp.int32, (2 * tqs, 2 * tqs), 1)
    eye = jnp.where(e_r == e_c, 1.0, 0.0).astype(BF16)
    sel = _dot_nt(eye, sel_t).astype(BF16)

    def reset():
        acc_ref[...] = jnp.zeros_like(acc_ref)
        l_ref[...] = jnp.zeros_like(l_ref)
        m_ref[...] = jnp.full_like(m_ref, NEG_INF)

    def attend(k_t, v_t, d_f, valid_of):
        s_all = _dot(qs, k_t.astype(BF16))
        es = []
        for r in range(NSA_HEADS):
            valid = valid_of(r)
            s = jnp.where(valid, s_all[rows(r)] - slopes[r] * d_f, NEG_INF)
            m_prev = m_ref[rows(r), :]
            m_new = jnp.maximum(m_prev, jnp.max(s, axis=-1, keepdims=True))
            alpha = jnp.exp(m_prev - m_new)
            e = jnp.where(valid, jnp.exp(s - m_new), 0.0)
            l_ref[rows(r), :] = alpha * l_ref[rows(r), :] + jnp.sum(e, axis=-1, keepdims=True)
            m_ref[rows(r), :] = m_new
            acc_ref[rows(r), :] = alpha * acc_ref[rows(r), :]
            es.append(e)
        acc_ref[...] += _dot_nt(jnp.concatenate(es, axis=0).astype(BF16), v_t.astype(BF16))

    def finish(branch):
        for r in range(NSA_HEADS):
            l = l_ref[rows(r), :]
            inv = jnp.where(l > 0.0, 1.0 / jnp.where(l > 0.0, l, 1.0), 0.0)
            out_ref[rows(r), :] += gate(r, branch) * (acc_ref[rows(r), :] * inv)

    eb_m = lax.broadcasted_iota(jnp.int32, (nbp, KEY_BLOCK), 0)
    eb_l = lax.broadcasted_iota(jnp.int32, (nbp, KEY_BLOCK), 1)
    eb_off = jnp.where(eb_l >= SEL_BLOCK, 1, 0)

    def sel_step(kb, k_t, v_t):
        expand = jnp.where(eb_m == 2 * kb + eb_off, 1.0, 0.0).astype(BF16)
        member = _dot(sel, expand)
        d = t_col - (kb * KEY_BLOCK + lane)
        causal = d >= 0
        attend(k_t, v_t, d.astype(F32),
               lambda r: causal & (member[(r // 4) * tqs:(r // 4) * tqs + tq] > 0.5))

    def sel_body(kb, carry):
        sel_step(kb, *sel_block(kb))
        return carry

    reset()
    lax.fori_loop(0, n_sel_blocks, sel_body, 0)
    if sel_tail is not None:
        sel_step(sel_tail[2], sel_tail[0], sel_tail[1])
    finish(1)

    def win_body(wb, carry):
        k0 = pl.multiple_of(wb * KEY_BLOCK, KEY_BLOCK)
        d = t_col - (win_start + wb * KEY_BLOCK + lane)
        valid = (d >= 0) & (d < WINDOW)
        attend(kw_ref[0, :, pl.ds(k0, KEY_BLOCK)], vw_ref[0, :, pl.ds(k0, KEY_BLOCK)], d.astype(F32),
               lambda r: valid)
        return carry

    reset()
    w_first = jnp.maximum(t0 - (WINDOW - 1) - win_start, 0) // KEY_BLOCK
    w_last = (t0 + tq - 1 - win_start) // KEY_BLOCK
    lax.fori_loop(w_first, w_last + 1, win_body, 0)
    finish(2)

    for ti in range(4):
        o_ref[0, :, ti * LANES:(ti + 1) * LANES] = jnp.where(lo, out_ref[rows(ti), :], out_ref[rows(4 + ti), :])


def _nsa_kernel(q_ref, g_ref, c_ref, ks_ref, vs_ref, kw_ref, vw_ref, o_ref, *scratch, tq, **static):
    i = pl.program_id(1)

    def sel_block(kb):
        k0 = pl.multiple_of(kb * KEY_BLOCK, KEY_BLOCK)
        return ks_ref[0, :, pl.ds(k0, KEY_BLOCK)], vs_ref[0, :, pl.ds(k0, KEY_BLOCK)]

    _nsa_core(q_ref, g_ref, c_ref, kw_ref, vw_ref, o_ref, *scratch, i=i, sel_block=sel_block,
              n_sel_blocks=((i + 1) * tq + KEY_BLOCK - 1) // KEY_BLOCK, sel_tail=None, tq=tq, **static)


def _nsa_paged_kernel(pt_ref, cache_ref, q_ref, g_ref, c_ref, tail_ref, kw_ref, vw_ref, o_ref, buf_ref, sem_ref,
                      *scratch, **static):
    n_pages = buf_ref.shape[1]
    slot = _stream_pages(pt_ref, cache_ref, buf_ref, sem_ref, pl.program_id(0), pl.num_programs(0),
                         lambda step: (step, 0))
    _nsa_core(q_ref, g_ref, c_ref, kw_ref, vw_ref, o_ref, *scratch, i=0,
              sel_block=lambda kb: (buf_ref[slot, kb, 0:LANES, :], buf_ref[slot, kb, LANES:2 * LANES, :]),
              n_sel_blocks=n_pages,
              sel_tail=(tail_ref[0, 0:LANES, :], tail_ref[0, LANES:2 * LANES, :], n_pages), **static)


def _nsa_statics(tq, q_start, win_start, n_keys):
    n_cmp = (n_keys - CMP_LEN) // CMP_STRIDE + 1
    n_blk = -(-n_keys // SEL_BLOCK)
    nbp = -(-n_blk // LANES) * LANES
    tqs = max(tq, LANES)
    scratch = [pltpu.VMEM((NSA_HEADS * tq, LANES), F32),
               pltpu.VMEM((NSA_HEADS * tq, LANES), F32),
               pltpu.VMEM((NSA_HEADS * tq, LANES), F32),
               pltpu.VMEM((NSA_HEADS * tq, LANES), F32),
               pltpu.VMEM((nbp, 2 * tqs), F32)]
    return dict(tq=tq, q_start=q_start, win_start=win_start, n_cmp=n_cmp, n_blk=n_blk), scratch


def _nsa_attention(q, gates, cmp, sel_kv, win_kv):
    batch, tokens, _ = q.shape
    tq = min(tokens, KEY_BLOCK)
    ncp = cmp.shape[1]
    static, scratch = _nsa_statics(tq, 0, 0, tokens)
    return pl.pallas_call(
        functools.partial(_nsa_kernel, **static),
        grid=(batch, tokens // tq),
        in_specs=[pl.BlockSpec((1, tq, 512), lambda b, i: (b, i, 0)),
                  pl.BlockSpec((1, tq, LANES), lambda b, i: (b, i, 0)),
                  pl.BlockSpec((1, ncp, 256), lambda b, i: (b, 0, 0)),
                  pl.BlockSpec((1, LANES, tokens), lambda b, i: (b, 0, 0)),
                  pl.BlockSpec((1, LANES, tokens), lambda b, i: (b, 1, 0)),
                  pl.BlockSpec((1, LANES, tokens), lambda b, i: (b, 0, 0)),
                  pl.BlockSpec((1, LANES, tokens), lambda b, i: (b, 1, 0))],
        out_specs=pl.BlockSpec((1, tq, 512), lambda b, i: (b, i, 0)),
        out_shape=jax.ShapeDtypeStruct((batch, tokens, 512), F32),
        scratch_shapes=scratch,
        compiler_params=_params(("parallel", "arbitrary")),
        name="nsa_attention",
    )(q, gates, cmp, sel_kv, sel_kv, win_kv, win_kv)


def _nsa_attention_paged(q, gates, cmp, page_table, sel_cache, sel_tail, win_kv, *, win_start):
    batch, tokens, _ = q.shape
    n_pages = page_table.shape[1]
    ncp, twp = cmp.shape[1], win_kv.shape[2]
    static, scratch = _nsa_statics(tokens, n_pages * PAGE, win_start, n_pages * PAGE + tokens)
    in_specs = [pl.BlockSpec((1, tokens, 512), lambda b, *_: (b, 0, 0)),
                pl.BlockSpec((1, tokens, LANES), lambda b, *_: (b, 0, 0)),
                pl.BlockSpec((1, ncp, 256), lambda b, *_: (b, 0, 0)),
                pl.BlockSpec((1, 256, PAGE), lambda b, *_: (b, 0, 0)),
                pl.BlockSpec((1, LANES, twp), lambda b, *_: (b, 0, 0)),
                pl.BlockSpec((1, LANES, twp), lambda b, *_: (b, 1, 0))]
    return _paged_call(functools.partial(_nsa_paged_kernel, **static), (batch,), page_table, sel_cache,
                       (q, gates, cmp, sel_tail, win_kv, win_kv), in_specs,
                       pl.BlockSpec((1, tokens, 512), lambda b, *_: (b, 0, 0)),
                       jax.ShapeDtypeStruct((batch, tokens, 512), F32), n_pages, scratch, "nsa_attention_paged")


SB_GROUP = 4
SB_PAGE_CHUNK = 16


def _sb_queries(q):
    lo = lax.broadcasted_iota(jnp.int32, q.shape, 1) < HEAD_DIM
    q = q * ATTN_SCALE
    return jnp.concatenate([jnp.where(lo, q, 0.0), jnp.where(lo, 0.0, q)], axis=0).astype(BF16)


def _sb_later_and_total():
    u_r = lax.broadcasted_iota(jnp.int32, (KEY_BLOCK, 2 * KEY_BLOCK), 0)
    u_c = lax.broadcasted_iota(jnp.int32, (KEY_BLOCK, 2 * KEY_BLOCK), 1)
    return jnp.where((u_c >= KEY_BLOCK) | (u_r > u_c), 1.0, 0.0).astype(BF16)


def _sb_chunk(qs, k_t, v_t, mask, later_and_total, acc_ref, r_ref):
    width = k_t.shape[1]
    z = _dot(qs, k_t.astype(BF16))
    sp = jnp.maximum(z, 0.0) + jnp.log(1.0 + jnp.exp(-jnp.abs(z)))
    if mask is not None:
        sp = jnp.where(mask, sp, 0.0)
    hi, lo_part = _split_bf16(sp)
    run = r_ref[...]
    laters = []
    for g in reversed(range(width // KEY_BLOCK)):
        cols = slice(g * KEY_BLOCK, (g + 1) * KEY_BLOCK)
        sums = _dot(hi[:, cols], later_and_total) + _dot(lo_part[:, cols], later_and_total)
        laters.append(sums[:, :KEY_BLOCK] + run)
        run = run + sums[:, KEY_BLOCK:]
    later = laters[0] if len(laters) == 1 else jnp.concatenate(laters[::-1], axis=1)
    a = jnp.exp(z - sp - later)
    if mask is not None:
        a = jnp.where(mask, a, 0.0)
    acc_ref[...] += _dot_nt(a.astype(BF16), v_t.astype(BF16))
    r_ref[...] = run


def _sb_causal(rows, tq, width, q_pos0, k_pos0):
    q_idx = lax.broadcasted_iota(jnp.int32, (rows, width), 0)
    q_idx = jnp.where(q_idx >= tq, q_idx - tq, q_idx)
    return k_pos0 + lax.broadcasted_iota(jnp.int32, (rows, width), 1) < q_pos0 + q_idx


def _sb_kernel(q_ref, k_ref, v_ref, o_ref, acc_ref, r_ref, *, tq, group):
    i = pl.program_id(2)
    width = group * KEY_BLOCK
    qs = _sb_queries(q_ref[0])
    lt = _sb_later_and_total()

    def chunk(c, mask):
        k0 = pl.multiple_of(c * width, width)
        _sb_chunk(qs, k_ref[0, :, pl.ds(k0, width)], v_ref[0, :, pl.ds(k0, width)], mask, lt, acc_ref, r_ref)

    acc_ref[...] = jnp.zeros_like(acc_ref)
    r_ref[...] = jnp.zeros_like(r_ref)
    c_top = (i * tq) // width
    chunk(c_top, _sb_causal(2 * tq, tq, width, i * tq, c_top * width))

    def past_body(s, carry):
        chunk(c_top - 1 - s, None)
        return carry

    lax.fori_loop(0, c_top, past_body, 0)
    lo = lax.broadcasted_iota(jnp.int32, (tq, LANES), 1) < HEAD_DIM
    o_ref[0] = jnp.where(lo, acc_ref[0:tq, :], acc_ref[tq:2 * tq, :])


def _sb_attention(q, kv):
    batch, tokens, _ = q.shape
    tq = min(tokens, KEY_BLOCK)
    group = min(SB_GROUP, tokens // KEY_BLOCK)
    assert tokens % (group * KEY_BLOCK) == 0
    pairs = SB_HEADS // 2
    return pl.pallas_call(
        functools.partial(_sb_kernel, tq=tq, group=group),
        grid=(batch, pairs, tokens // tq),
        in_specs=[pl.BlockSpec((1, tq, LANES), lambda b, p, i: (b, i, p)),
                  pl.BlockSpec((1, LANES, tokens), lambda b, p, i: (b, p, 0)),
                  pl.BlockSpec((1, LANES, tokens), lambda b, p, i: (b, pairs + p, 0))],
        out_specs=pl.BlockSpec((1, tq, LANES), lambda b, p, i: (b, i, p)),
        out_shape=jax.ShapeDtypeStruct((batch, tokens, 512), F32),
        scratch_shapes=[pltpu.VMEM((2 * tq, LANES), F32), pltpu.VMEM((2 * tq, LANES), F32)],
        compiler_params=_params(("parallel", "parallel", "arbitrary")),
        name="stick_breaking_attention",
    )(q, kv, kv)


def _sb_paged_kernel(pt_ref, cache_ref, q_ref, tail_ref, o_ref, buf_ref, sem_ref, acc_ref, r_ref, *, tq, n_pages):
    pairs = SB_HEADS // 2
    chunk_pages = buf_ref.shape[1]
    n_chunks = n_pages // chunk_pages
    c = pl.program_id(1)
    step = pl.program_id(0) * n_chunks + c
    slot = _stream_pages(pt_ref, cache_ref, buf_ref, sem_ref, step, pl.num_programs(0) * n_chunks,
                         lambda s: (s // n_chunks, n_pages - (s % n_chunks + 1) * chunk_pages))
    qs = [_sb_queries(q_ref[0, :, p * LANES:(p + 1) * LANES]) for p in range(pairs)]
    lt = _sb_later_and_total()

    def k_rows(p):
        return slice(p * LANES, (p + 1) * LANES)

    def v_rows(p):
        return slice((pairs + p) * LANES, (pairs + p + 1) * LANES)

    @pl.when(c == 0)
    def _():
        acc_ref[...] = jnp.zeros_like(acc_ref)
        r_ref[...] = jnp.zeros_like(r_ref)
        mask = _sb_causal(2 * tq, tq, PAGE, 0, 0)
        for p in range(pairs):
            _sb_chunk(qs[p], tail_ref[0, k_rows(p), :], tail_ref[0, v_rows(p), :], mask, lt,
                      acc_ref.at[p], r_ref.at[p])

    group = min(SB_GROUP, chunk_pages)

    def group_body(s, carry):
        pg0 = chunk_pages - (s + 1) * group
        for p in range(pairs):
            k_t = jnp.concatenate([buf_ref[slot, pg0 + j, k_rows(p), :] for j in range(group)], axis=1)
            v_t = jnp.concatenate([buf_ref[slot, pg0 + j, v_rows(p), :] for j in range(group)], axis=1)
            _sb_chunk(qs[p], k_t, v_t, None, lt, acc_ref.at[p], r_ref.at[p])
        return carry

    lax.fori_loop(0, chunk_pages // group, group_body, 0)

    @pl.when(c == n_chunks - 1)
    def _():
        lo = lax.broadcasted_iota(jnp.int32, (tq, LANES), 1) < HEAD_DIM
        for p in range(pairs):
            o_ref[0, :, p * LANES:(p + 1) * LANES] = jnp.where(lo, acc_ref[p, 0:tq, :], acc_ref[p, tq:2 * tq, :])


def _sb_attention_paged(q, page_table, cache, tail):
    batch, tokens, _ = q.shape
    n_pages = page_table.shape[1]
    chunk_pages = min(SB_PAGE_CHUNK, n_pages)
    assert n_pages % chunk_pages == 0 and chunk_pages % min(SB_GROUP, chunk_pages) == 0
    pairs = SB_HEADS // 2
    in_specs = [pl.BlockSpec((1, tokens, 512), lambda b, c, *_: (b, 0, 0)),
                pl.BlockSpec((1, 1024, PAGE), lambda b, c, *_: (b, 0, 0))]
    return _paged_call(functools.partial(_sb_paged_kernel, tq=tokens, n_pages=n_pages),
                       (batch, n_pages // chunk_pages), page_table, cache, (q, tail), in_specs,
                       pl.BlockSpec((1, tokens, 512), lambda b, c, *_: (b, 0, 0)),
                       jax.ShapeDtypeStruct((batch, tokens, 512), F32), chunk_pages,
                       [pltpu.VMEM((pairs, 2 * tokens, LANES), F32), pltpu.VMEM((pairs, 2 * tokens, LANES), F32)],
                       "stick_breaking_attention_paged")


def _prep_weights(ffn1_w_in, ffn1_w_out, ffn2_w_in, ffn2_w_out, w_in, nsa_q_norm, nsa_k_norm,
                  cmp_pe, cmp_w1, cmp_w2, w_up_nsa, w_up_sb, w_out):
    w_perm = jnp.concatenate([w_in[:, _QPERM], w_in[:, 1304:1816], w_in[:, 2840:4888], w_in[:, 1280:1304],
                              jnp.zeros((D_MODEL, _C_END - _C_GN - 24), F32)], axis=1).astype(BF16)
    w_kv_t = jnp.concatenate([w_in[:, 512:1280], w_in[:, 1816:2840]], axis=1).T.astype(BF16)
    tile2 = lambda g: jnp.tile(g, 2).reshape(1, LANES)
    eye2 = jnp.eye(NSA_KV, dtype=F32)

    def big_w1(kv, half):
        w = cmp_w1[kv, half * CMP_STRIDE:(half + 1) * CMP_STRIDE]
        full = jnp.einsum('lde,gf->lgdfe', w, eye2)
        return full.reshape(CMP_STRIDE * LANES, LANES).astype(BF16)

    def pe_row(kv, half):
        pe = cmp_pe[kv, half * CMP_STRIDE:(half + 1) * CMP_STRIDE]
        return jnp.tile(pe[:, None, :], (1, NSA_KV, 1)).reshape(1, CMP_STRIDE * LANES)

    order = [(kv, half) for kv in range(2) for half in range(2)]
    w2_big = jnp.stack([jnp.einsum('de,gf->gdfe', cmp_w2[kv], eye2).reshape(LANES, LANES) for kv in range(2)])
    return dict(
        f1_in=ffn1_w_in.astype(BF16), f1_out=ffn1_w_out.astype(BF16),
        f2_in=ffn2_w_in.astype(BF16), f2_out=ffn2_w_out.astype(BF16),
        w_perm=w_perm, w_kv_t=w_kv_t, q_gain=tile2(nsa_q_norm),
        k_gain_cmp=tile2(nsa_k_norm[0]),
        k_gain2=jnp.stack([jnp.tile(nsa_k_norm[1], 2), jnp.tile(nsa_k_norm[2], 2)]).reshape(2, LANES, 1),
        pe4=jnp.concatenate([pe_row(kv, half) for kv, half in order], axis=0),
        w1_big=jnp.stack([big_w1(kv, half) for kv, half in order]), w2_big=w2_big.astype(BF16),
        w_up_nsa=w_up_nsa[_QPERM].astype(BF16), w_up_sb=w_up_sb.astype(BF16), w_out=w_out.astype(BF16))


def _pad_tokens(x, n):
    return jnp.pad(x, ((0, 0), (0, 0), (0, n - x.shape[2])))


def _per_sequence(x_t, batch, tokens):
    if x_t.shape[0] == batch:
        return x_t
    return x_t.reshape(x_t.shape[1], batch, tokens).transpose(1, 0, 2)


def _layer(x, mod, past, norms, w):
    batch, tokens, _ = x.shape
    n1, nm, n2 = norms
    x = _ffn(x, mod, 0, n1, w['f1_in'], w['f1_out'])
    q, sq, g_mn, g_ms, g_nsa, new_cmp, new_sel, win_rows, new_sb = _projection(
        x, mod, nm, w['w_perm'], w['w_kv_t'], w['q_gain'], w['k_gain2'])
    new_cmp, new_sel, win_rows, new_sb = (_per_sequence(a, batch, tokens) for a in (new_cmp, new_sel, win_rows, new_sb))
    cmp_w = (w['pe4'], w['w1_big'], w['w2_big'], w['k_gain_cmp'])
    if past is None:
        assert tokens % LANES == 0
        new_win = win_rows[:, :, tokens - min(WINDOW, tokens):]
        cmp = _compress(new_cmp, *cmp_w)
        o_nsa = _nsa_attention(q, g_nsa, cmp, new_sel, win_rows)
        o_sb = _sb_attention(sq, new_sb)
    else:
        page_table, cmp_cache, sel_cache, sb_cache, win_buf = past
        q_start = page_table.shape[1] * PAGE
        assert tokens < PAGE and (q_start + tokens - CMP_LEN) // CMP_STRIDE + 2 == q_start // CMP_STRIDE
        win_cat = jnp.concatenate([win_buf, win_rows], axis=2)
        new_win = win_cat[:, :, win_cat.shape[2] - min(WINDOW, win_cat.shape[2]):]
        win_all = _pad_tokens(win_cat, -(-win_cat.shape[2] // KEY_BLOCK) * KEY_BLOCK)
        cmp = _compress_paged(page_table, cmp_cache, *cmp_w)
        o_nsa = _nsa_attention_paged(q, g_nsa, cmp, page_table, sel_cache, _pad_tokens(new_sel, PAGE), win_all,
                                     win_start=q_start - win_buf.shape[2])
        o_sb = _sb_attention_paged(sq, page_table, sb_cache, _pad_tokens(new_sb, PAGE))
    x = _merge(x, mod, o_nsa, o_sb, g_mn, g_ms, w['w_up_nsa'], w['w_up_sb'], w['w_out'])
    x = _ffn(x, mod, 6, n2, w['f2_in'], w['f2_out'])
    return x, new_cmp, new_sel, new_sb, new_win


def kernel(x_prompt, x_sample, c_prompt, c_sample, cache_nsa_cmp, cache_nsa_sel, cache_sb, state_nsa_win, page_table, norm_ffn1, norm_mix, norm_ffn2, w_ada, b_ada, ffn1_w_in, ffn1_w_out, ffn2_w_in, ffn2_w_out, w_in, nsa_q_norm, nsa_k_norm, cmp_pe, cmp_w1, cmp_w2, w_up_nsa, w_up_sb, w_out):
    depth = w_in.shape[0]
    assert depth == 1, "single-layer trunk"
    bp, tp, _ = x_prompt.shape
    bs, ts, _ = x_sample.shape
    n_pool = cache_nsa_cmp.shape[1]
    w = _prep_weights(ffn1_w_in[0], ffn1_w_out[0], ffn2_w_in[0], ffn2_w_out[0], w_in[0], nsa_q_norm[0],
                      nsa_k_norm[0], cmp_pe[0], cmp_w1[0], cmp_w2[0], w_up_nsa[0], w_up_sb[0], w_out[0])
    norms = (norm_ffn1[0], norm_mix[0], norm_ffn2[0])
    mod = _modulation(jnp.concatenate([c_prompt, c_sample], axis=0), w_ada[0], b_ada[0])
    mod_p = mod[:bp].reshape(bp, 1, 9 * D_MODEL)
    mod_s = mod[bp:].reshape(bs, 1, 9 * D_MODEL)

    yp, cmp_p, sel_p, sb_p, win_p = _layer(x_prompt, mod_p, None, norms, w)

    def pages_t(cache):
        return cache[0].transpose(0, 2, 3, 4, 1).reshape(n_pool, -1, PAGE)

    win_buf = state_nsa_win[0].transpose(0, 2, 3, 4, 1).reshape(bs, 256, state_nsa_win.shape[2])
    past = (page_table, pages_t(cache_nsa_cmp), pages_t(cache_nsa_sel), pages_t(cache_sb), win_buf)
    ys, cmp_s, sel_s, sb_s, win_s = _layer(x_sample, mod_s, past, norms, w)

    def rows(a, heads):
        return a.reshape(1, a.shape[0], 2, heads, HEAD_DIM, a.shape[2]).transpose(0, 1, 5, 2, 3, 4)

    return (yp, ys, rows(cmp_p, NSA_KV), rows(sel_p, NSA_KV), rows(sb_p, SB_HEADS), rows(win_p, NSA_KV),
            rows(cmp_s, NSA_KV), rows(sel_s, NSA_KV), rows(sb_s, SB_HEADS), rows(win_s, NSA_KV))
```

```python
import functools

import numpy as np
import jax
import jax.numpy as jnp
from jax import lax
from jax.experimental import pallas as pl
from jax.experimental.pallas import tpu as pltpu

F32 = jnp.float32
BF16 = jnp.bfloat16

D_MODEL = 1024
HEAD_DIM = 64
NSA_HEADS = 8
NSA_KV = 2
SB_HEADS = 8
CMP_LEN = 32
CMP_STRIDE = 16
SEL_BLOCK = 64
SEL_TOPK = 16
WINDOW = 512
D_FF = 2816
EPS = 1e-6
NEG_INF = -1.0e30
FORCE_SCORE = 1.0e6
PAGE = 128

LANES = 128
KEY_BLOCK = 128
VMEM_LIMIT = 56 * 1024 * 1024
ATTN_SCALE = HEAD_DIM ** -0.5

_QPERM = np.concatenate([np.concatenate([np.arange(64 * i, 64 * i + 64), np.arange(64 * (4 + i), 64 * (4 + i) + 64)])
                         for i in range(4)])
_C_Q, _C_SQ, _C_GMN, _C_GMS, _C_GN, _C_END = 0, 512, 1024, 2048, 3072, 3200
_R_CMP, _R_SEL, _R_WIN, _R_SB, _R_END = 0, 256, 512, 768, 1792


def _params(sem, vmem=VMEM_LIMIT):
    return pltpu.CompilerParams(dimension_semantics=sem, vmem_limit_bytes=vmem)


def _dot(a, b):
    return jnp.dot(a, b, preferred_element_type=F32)


def _dot_nt(a, b):
    return lax.dot_general(a, b, (((1,), (1,)), ((), ())), preferred_element_type=F32)


def _split_bf16(x):
    hi = x.astype(BF16)
    lo = (x - hi.astype(F32)).astype(BF16)
    return hi, lo


def _silu(x):
    return x * jax.nn.sigmoid(x)


def _norm_pairs(x, gain):
    outs = []
    for c in range(x.shape[1] // LANES):
        xt = x[:, c * LANES:(c + 1) * LANES]
        lo = lax.broadcasted_iota(jnp.int32, xt.shape, 1) < HEAD_DIM
        sq = xt * xt
        ms_lo = jnp.sum(jnp.where(lo, sq, 0.0), axis=-1, keepdims=True) * (1.0 / HEAD_DIM)
        ms_hi = jnp.sum(jnp.where(lo, 0.0, sq), axis=-1, keepdims=True) * (1.0 / HEAD_DIM)
        inv = jnp.where(lo, lax.rsqrt(ms_lo + EPS), lax.rsqrt(ms_hi + EPS))
        outs.append(xt * inv * gain)
    return outs[0] if len(outs) == 1 else jnp.concatenate(outs, axis=1)


def _ada_norm(x, gain, shift, scale):
    ms = jnp.mean(x * x, axis=-1, keepdims=True)
    n = x * lax.rsqrt(ms + EPS) * gain
    h = n * (1.0 + scale) + shift
    return h.reshape(x.shape[0] * x.shape[1], x.shape[2]).astype(BF16)


def _mod_kernel(c_ref, w_ref, b_ref, o_ref):
    s = _silu(c_ref[...]).astype(BF16)
    o_ref[...] = _dot(s, w_ref[...].astype(BF16)) + b_ref[...]


def _modulation(c, w_ada, b_ada):
    rows = c.shape[0]
    n_out = w_ada.shape[1]
    tn = D_MODEL
    return pl.pallas_call(
        _mod_kernel,
        grid=(n_out // tn,),
        in_specs=[pl.BlockSpec((rows, D_MODEL), lambda j: (0, 0)),
                  pl.BlockSpec((D_MODEL, tn), lambda j: (0, j)),
                  pl.BlockSpec((1, tn), lambda j: (0, j))],
        out_specs=pl.BlockSpec((rows, tn), lambda j: (0, j)),
        out_shape=jax.ShapeDtypeStruct((rows, n_out), F32),
        compiler_params=_params(("arbitrary",)),
        name="adaln_modulation",
    )(c, w_ada, b_ada.reshape(1, n_out))


def _ffn_kernel(x_ref, sh_ref, sc_ref, gt_ref, g_ref, wi_ref, wo_ref, o_ref, *, chunk):
    x = x_ref[...]
    hb = _ada_norm(x, g_ref[...], sh_ref[...], sc_ref[...])
    acc = None
    for c0 in range(0, D_FF, chunk):
        a = _dot(hb, wi_ref[:, c0:c0 + chunk])
        b = _dot(hb, wi_ref[:, D_FF + c0:D_FF + c0 + chunk])
        act = (_silu(a) * b).astype(BF16)
        part = _dot(act, wo_ref[c0:c0 + chunk, :])
        acc = part if acc is None else acc + part
    o_ref[...] = x + 0.5 * gt_ref[...] * acc.reshape(x.shape)


def _row_tiles(batch, tokens, rows=512):
    if tokens >= rows:
        return 1, rows
    bb = max(1, min(batch, rows // tokens))
    while batch % bb:
        bb -= 1
    return bb, tokens


def _mod_spec(bb, k):
    return pl.BlockSpec((bb, 1, D_MODEL), lambda i, j, k=k: (i, 0, k))


def _const_spec(shape):
    nd = len(shape)
    return pl.BlockSpec(shape, lambda i, j: (0,) * nd, pipeline_mode=pl.Buffered(1))


def _ffn(x, mod, k0, gain, w_in, w_out):
    batch, tokens, _ = x.shape
    bb, tt = _row_tiles(batch, tokens)
    xspec = pl.BlockSpec((bb, tt, D_MODEL), lambda i, j: (i, j, 0))
    return pl.pallas_call(
        functools.partial(_ffn_kernel, chunk=D_FF // 2),
        grid=(batch // bb, tokens // tt),
        in_specs=[xspec, _mod_spec(bb, k0), _mod_spec(bb, k0 + 1), _mod_spec(bb, k0 + 2),
                  _const_spec((1, D_MODEL)), _const_spec((D_MODEL, 2 * D_FF)), _const_spec((D_FF, D_MODEL))],
        out_specs=xspec,
        out_shape=jax.ShapeDtypeStruct(x.shape, F32),
        compiler_params=_params(("parallel", "parallel")),
        name="ffn_swiglu",
    )(x, mod, mod, mod, gain.reshape(1, D_MODEL), w_in, w_out)


def _norm_groups_t(x, gain_col):
    outs = []
    for g in range(NSA_KV):
        xg = x[g * HEAD_DIM:(g + 1) * HEAD_DIM]
        ms = jnp.mean(xg * xg, axis=0, keepdims=True)
        outs.append(xg * lax.rsqrt(ms + EPS) * gain_col[g * HEAD_DIM:(g + 1) * HEAD_DIM])
    return jnp.concatenate(outs, axis=0)


def _proj_kernel(x_ref, sh_ref, sc_ref, g_ref, w_ref, wt_ref, qg_ref, kg_ref,
                 q_o, sq_o, gmn_o, gms_o, gn_o, cmp_o, sel_o, win_o, sb_o):
    x = x_ref[...]
    bb, tt, _ = x.shape
    hb = _ada_norm(x, g_ref[...], sh_ref[...], sc_ref[...])

    def cols(c0, c1):
        return _dot(hb, w_ref[:, c0:c1])

    def rows_t(r0, r1):
        return _dot_nt(wt_ref[r0:r1, :], hb)

    def put(ref, val):
        ref[...] = val.reshape(bb, tt, val.shape[-1])

    put(q_o, _norm_pairs(cols(_C_Q, _C_SQ), qg_ref[...]))
    put(sq_o, cols(_C_SQ, _C_GMN))
    put(gmn_o, jax.nn.sigmoid(cols(_C_GMN, _C_GMS)))
    put(gms_o, jax.nn.sigmoid(cols(_C_GMS, _C_GN)))
    put(gn_o, jax.nn.sigmoid(cols(_C_GN, _C_END)))
    cmp_o[0] = rows_t(_R_CMP, _R_SEL)
    sel_o[0, 0:LANES, :] = _norm_groups_t(rows_t(_R_SEL, _R_SEL + LANES), kg_ref[0])
    sel_o[0, LANES:2 * LANES, :] = rows_t(_R_SEL + LANES, _R_WIN)
    win_o[0, 0:LANES, :] = _norm_groups_t(rows_t(_R_WIN, _R_WIN + LANES), kg_ref[1])
    win_o[0, LANES:2 * LANES, :] = rows_t(_R_WIN + LANES, _R_SB)
    sb_o[0] = rows_t(_R_SB, _R_END)


def _projection(x, mod, gain, w_perm, w_kv_t, q_gain, k_gain2):
    batch, tokens, _ = x.shape
    bb, tt = _row_tiles(batch, tokens)
    widths = (512, 512, 1024, 1024, 128)
    feats = (256, 256, 256, 1024)

    def rspec(w):
        return pl.BlockSpec((bb, tt, w), lambda i, j: (i, j, 0))

    if bb == 1:
        t_shape = lambda f: (batch, f, tokens)
        tspec = lambda f: pl.BlockSpec((1, f, tt), lambda i, j: (i, 0, j))
    else:
        assert tt == tokens
        t_shape = lambda f: (1, f, batch * tokens)
        tspec = lambda f: pl.BlockSpec((1, f, bb * tt), lambda i, j: (0, 0, i))

    return pl.pallas_call(
        _proj_kernel,
        grid=(batch // bb, tokens // tt),
        in_specs=[rspec(D_MODEL), _mod_spec(bb, 3), _mod_spec(bb, 4),
                  _const_spec((1, D_MODEL)), _const_spec((D_MODEL, _C_END)), _const_spec((_R_END, D_MODEL)),
                  _const_spec((1, LANES)), _const_spec((2, LANES, 1))],
        out_specs=[rspec(w) for w in widths] + [tspec(f) for f in feats],
        out_shape=([jax.ShapeDtypeStruct((batch, tokens, w), F32) for w in widths]
                   + [jax.ShapeDtypeStruct(t_shape(f), F32) for f in feats]),
        compiler_params=_params(("parallel", "parallel")),
        name="mixer_in_proj",
    )(x, mod, mod, gain.reshape(1, D_MODEL), w_perm, w_kv_t, q_gain, k_gain2)


def _merge_kernel(x_ref, gt_ref, on_ref, os_ref, gmn_ref, gms_ref, wun_ref, wus_ref, wo_ref, o_ref):
    x = x_ref[...]
    rows = x.shape[0] * x.shape[1]

    def flat(ref):
        v = ref[...]
        return v.reshape(rows, v.shape[-1])

    y = (flat(gmn_ref) * _dot(flat(on_ref).astype(BF16), wun_ref[...])
         + flat(gms_ref) * _dot(flat(os_ref).astype(BF16), wus_ref[...]))
    m = _dot(y.astype(BF16), wo_ref[...])
    o_ref[...] = x + gt_ref[...] * m.reshape(x.shape)


def _merge(x, mod, o_nsa, o_sb, g_mn, g_ms, w_up_nsa, w_up_sb, w_out):
    batch, tokens, _ = x.shape
    bb, tt = _row_tiles(batch, tokens)

    def rspec(w):
        return pl.BlockSpec((bb, tt, w), lambda i, j: (i, j, 0))

    return pl.pallas_call(
        _merge_kernel,
        grid=(batch // bb, tokens // tt),
        in_specs=[rspec(D_MODEL), _mod_spec(bb, 5), rspec(512), rspec(512), rspec(D_MODEL), rspec(D_MODEL),
                  _const_spec((512, D_MODEL)), _const_spec((512, D_MODEL)), _const_spec((D_MODEL, D_MODEL))],
        out_specs=rspec(D_MODEL),
        out_shape=jax.ShapeDtypeStruct(x.shape, F32),
        compiler_params=_params(("parallel", "parallel")),
        name="mixer_out_merge",
    )(x, mod, o_nsa, o_sb, g_mn, g_ms, w_up_nsa, w_up_sb, w_out)


def _stream_pages(pt_ref, cache_ref, buf_ref, sem_ref, step, n_steps, pages_of):
    n = buf_ref.shape[1]

    def copies(at_step, slot):
        seq, page0 = pages_of(at_step)

        def copy(j):
            return pltpu.make_async_copy(cache_ref.at[pt_ref[seq, page0 + j]], buf_ref.at[slot, j], sem_ref.at[slot])
        return copy

    def start(at_step, slot):
        copy = copies(at_step, slot)
        lax.fori_loop(0, n, lambda j, c: (copy(j).start(), c)[1], 0)

    slot = step % 2

    @pl.when(step == 0)
    def _():
        start(step, slot)

    @pl.when(step + 1 < n_steps)
    def _():
        start(step + 1, 1 - slot)

    copy = copies(step, slot)
    lax.fori_loop(0, n, lambda j, c: (copy(j).wait(), c)[1], 0)
    return slot


def _paged_call(body, grid, page_table, cache, operands, in_specs, out_specs, out_shape, pages_per_step,
                scratch_shapes, name):
    return pl.pallas_call(
        body,
        grid_spec=pltpu.PrefetchScalarGridSpec(
            num_scalar_prefetch=1,
            grid=grid,
            in_specs=[pl.BlockSpec(memory_space=pl.ANY)] + in_specs,
            out_specs=out_specs,
            scratch_shapes=[pltpu.VMEM((2, pages_per_step) + cache.shape[1:], F32),
                            pltpu.SemaphoreType.DMA((2,))] + scratch_shapes),
        out_shape=out_shape,
        compiler_params=_params(("arbitrary",) * len(grid)),
        name=name,
    )(page_table, cache, *operands)


def _cmp_core(load_chunk, pe_ref, w1_ref, w2_ref, kg_ref, o_ref, tok_ref):
    n_tok = tok_ref.shape[1]
    rows = n_tok // CMP_STRIDE

    def to_token_major(c, carry):
        c0 = pl.multiple_of(c * LANES, LANES)
        xt = load_chunk(c).T
        tok_ref[0, pl.ds(c0, LANES), :] = xt[:, 0:LANES]
        tok_ref[1, pl.ds(c0, LANES), :] = xt[:, LANES:2 * LANES]
        return carry

    lax.fori_loop(0, n_tok // LANES, to_token_major, 0)
    for kv in range(2):
        x = jnp.concatenate([tok_ref[kv, pl.ds(l, rows, stride=CMP_STRIDE), :] for l in range(CMP_STRIDE)], axis=1)
        a = _dot((x + pe_ref[2 * kv:2 * kv + 1, :]).astype(BF16), w1_ref[2 * kv])
        b = _dot((x + pe_ref[2 * kv + 1:2 * kv + 2, :]).astype(BF16), w1_ref[2 * kv + 1])
        h1 = a + pltpu.roll(b, rows - 1, axis=0)
        c = _dot(_silu(h1).astype(BF16), w2_ref[kv])
        o_ref[0, :, kv * LANES:(kv + 1) * LANES] = _norm_pairs(c, kg_ref[...]) if kv == 0 else c


def _cmp_kernel(x_ref, pe_ref, w1_ref, w2_ref, kg_ref, o_ref, tok_ref):
    _cmp_core(lambda c: x_ref[0, :, pl.ds(pl.multiple_of(c * LANES, LANES), LANES)],
              pe_ref, w1_ref, w2_ref, kg_ref, o_ref, tok_ref)


def _cmp_paged_kernel(pt_ref, cache_ref, pe_ref, w1_ref, w2_ref, kg_ref, o_ref, buf_ref, sem_ref, tok_ref):
    slot = _stream_pages(pt_ref, cache_ref, buf_ref, sem_ref, pl.program_id(0), pl.num_programs(0),
                         lambda step: (step, 0))
    _cmp_core(lambda c: buf_ref[slot, c], pe_ref, w1_ref, w2_ref, kg_ref, o_ref, tok_ref)


def _cmp_specs(rows):
    width = CMP_STRIDE * LANES
    in_specs = [pl.BlockSpec((4, width), lambda b, *_: (0, 0)),
                pl.BlockSpec((4, width, LANES), lambda b, *_: (0, 0, 0)),
                pl.BlockSpec((2, LANES, LANES), lambda b, *_: (0, 0, 0)),
                pl.BlockSpec((1, LANES), lambda b, *_: (0, 0))]
    return in_specs, pl.BlockSpec((1, rows, 256), lambda b, *_: (b, 0, 0))


def _compress(x_t, pe4, w1_big, w2_big, k_gain):
    batch, feats, n_tok = x_t.shape
    rows = n_tok // CMP_STRIDE
    in_specs, out_spec = _cmp_specs(rows)
    return pl.pallas_call(
        _cmp_kernel,
        grid=(batch,),
        in_specs=[pl.BlockSpec((1, feats, n_tok), lambda b: (b, 0, 0))] + in_specs,
        out_specs=out_spec,
        out_shape=jax.ShapeDtypeStruct((batch, rows, 256), F32),
        scratch_shapes=[pltpu.VMEM((2, n_tok, LANES), F32)],
        compiler_params=_params(("parallel",)),
        name="nsa_compress",
    )(x_t, pe4, w1_big, w2_big, k_gain)


def _compress_paged(page_table, cache, pe4, w1_big, w2_big, k_gain):
    batch, n_pages = page_table.shape
    n_tok = n_pages * PAGE
    rows = n_tok // CMP_STRIDE
    in_specs, out_spec = _cmp_specs(rows)
    return _paged_call(_cmp_paged_kernel, (batch,), page_table, cache, (pe4, w1_big, w2_big, k_gain), in_specs,
                       out_spec, jax.ShapeDtypeStruct((batch, rows, 256), F32), n_pages,
                       [pltpu.VMEM((2, n_tok, LANES), F32)], "nsa_compress_paged")


def _nsa_core(q_ref, g_ref, c_ref, kw_ref, vw_ref, o_ref, acc_ref, m_ref, l_ref, out_ref, imp_ref,
              *, i, sel_chunk, sel_group, n_sel_chunks, sel_tail, tq, q_start, win_start, n_cmp, n_blk):
    ncp = c_ref.shape[1]
    nbp = imp_ref.shape[0]
    tqs = imp_ref.shape[1] // 2
    t0 = q_start + i * tq
    slopes = [2.0 ** (-(h + 1)) for h in range(NSA_HEADS)]

    lane = lax.broadcasted_iota(jnp.int32, (tq, LANES), 1)
    lo = lane < HEAD_DIM
    t_col = t0 + lax.broadcasted_iota(jnp.int32, (tq, LANES), 0)

    tiles = [q_ref[0, :, ti * LANES:(ti + 1) * LANES] * ATTN_SCALE for ti in range(4)]
    qs = jnp.concatenate([jnp.where(lo, t, 0.0) for t in tiles] + [jnp.where(lo, 0.0, t) for t in tiles],
                         axis=0).astype(BF16)
    gates = g_ref[0]

    def gate(r, branch):
        return gates[:, 3 * r + branch:3 * r + branch + 1]

    def rows(r):
        return slice(r * tq, (r + 1) * tq)

    kc = c_ref[0, :, 0:LANES].astype(BF16)
    vc = c_ref[0, :, LANES:2 * LANES].astype(BF16)
    s_c = _dot_nt(qs, kc)
    n_idx = lax.broadcasted_iota(jnp.int32, (tq, ncp), 1)
    d_c = (t0 + lax.broadcasted_iota(jnp.int32, (tq, ncp), 0)) - (CMP_STRIDE * n_idx + (CMP_LEN - 1))
    valid_c = (d_c >= 0) & (n_idx < n_cmp)
    d_cf = d_c.astype(F32)
    probs = []
    for r in range(NSA_HEADS):
        s = jnp.where(valid_c, s_c[rows(r)] - slopes[r] * d_cf, NEG_INF)
        m = jnp.max(s, axis=-1, keepdims=True)
        e = jnp.where(valid_c, jnp.exp(s - m), 0.0)
        l = jnp.sum(e, axis=-1, keepdims=True)
        probs.append(e / jnp.where(l > 0.0, l, 1.0))
    o_c = _dot(jnp.concatenate(probs, axis=0).astype(BF16), vc)
    for r in range(NSA_HEADS):
        out_ref[rows(r), :] = gate(r, 0) * o_c[rows(r)]

    psum = []
    for g in range(NSA_KV):
        ps = probs[4 * g] + probs[4 * g + 1] + probs[4 * g + 2] + probs[4 * g + 3]
        if tqs > tq:
            ps = jnp.concatenate([ps, jnp.zeros((tqs - tq, ncp), F32)], axis=0)
        psum.append(ps)
    p_hi, p_lo = _split_bf16(jnp.concatenate(psum, axis=0))
    m_o = lax.broadcasted_iota(jnp.int32, (nbp, ncp), 0)
    n_o = lax.broadcasted_iota(jnp.int32, (nbp, ncp), 1)
    ov_t = jnp.where((n_o >= 4 * m_o - 1) & (n_o <= 4 * m_o + 3) & (n_o < n_cmp), 1.0, 0.0).astype(BF16)
    imp = _dot_nt(ov_t, p_hi) + _dot_nt(ov_t, p_lo)
    m_i = lax.broadcasted_iota(jnp.int32, (nbp, 2 * tqs), 0)
    q_i = lax.broadcasted_iota(jnp.int32, (nbp, 2 * tqs), 1)
    t_row = t0 + jnp.where(q_i >= tqs, q_i - tqs, q_i)
    blk_t = t_row >> 6
    forced = (m_i == 0) | (m_i == blk_t) | (m_i == blk_t - 1)
    future = (m_i * SEL_BLOCK > t_row) | (m_i >= n_blk)
    imp = jnp.where(future, -1.0, jnp.where(forced, FORCE_SCORE, imp))
    imp_ref[...] = imp

    def rank_body(j, cnt):
        row = imp_ref[pl.ds(j, 1), :]
        tie = jnp.where(m_i > j, 1.0, 0.0)
        return cnt + jnp.where(row > imp, 1.0, jnp.where(row == imp, tie, 0.0))

    cnt = lax.fori_loop(0, n_blk, rank_body, jnp.zeros((nbp, 2 * tqs), F32))
    sel_t = jnp.where(cnt < float(SEL_TOPK), jnp.where(imp >= 0.0, 1.0, 0.0), 0.0).astype(BF16)
    e_r = lax.broadcasted_iota(jnp.int32, (2 * tqs, 2 * tqs), 0)
    e_c = lax.broadcasted_iota(jnp.int32, (2 * tqs, 2 * tqs), 1)
    eye = jnp.where(e_r == e_c, 1.0, 0.0).astype(BF16)
    sel = _dot_nt(eye, sel_t).astype(BF16)

    def reset():
        acc_ref[...] = jnp.zeros_like(acc_ref)
        l_ref[...] = jnp.zeros_like(l_ref)
        m_ref[...] = jnp.full_like(m_ref, NEG_INF)

    def attend(k_t, v_t, d_f, valid_of):
        reps = k_t.shape[1] // LANES
        s_all = _dot(qs, k_t.astype(BF16))
        es = []
        for r in range(NSA_HEADS):
            valid = valid_of(r)
            s = jnp.where(valid, s_all[rows(r)] - slopes[r] * d_f, NEG_INF)
            m_prev = m_ref[rows(r), :]
            m_new = jnp.maximum(m_prev, jnp.max(s, axis=-1, keepdims=True))
            alpha = jnp.exp(m_prev - m_new)
            m_wide = m_new if reps == 1 else jnp.concatenate([m_new] * reps, axis=1)
            e = jnp.where(valid, jnp.exp(s - m_wide), 0.0)
            l_ref[rows(r), :] = alpha * l_ref[rows(r), :] + jnp.sum(e, axis=-1, keepdims=True)
            m_ref[rows(r), :] = m_new
            acc_ref[rows(r), :] = alpha * acc_ref[rows(r), :]
            es.append(e)
        acc_ref[...] += _dot_nt(jnp.concatenate(es, axis=0).astype(BF16), v_t.astype(BF16))

    def key_offsets(width):
        return (t0 + lax.broadcasted_iota(jnp.int32, (tq, width), 0)) - lax.broadcasted_iota(jnp.int32, (tq, width), 1)

    def finish(branch):
        for r in range(NSA_HEADS):
            l = l_ref[rows(r), :]
            inv = jnp.where(l > 0.0, 1.0 / jnp.where(l > 0.0, l, 1.0), 0.0)
            out_ref[rows(r), :] += gate(r, branch) * (acc_ref[rows(r), :] * inv)

    def sel_step(kb0, k_t, v_t):
        width = k_t.shape[1]
        eb_m = lax.broadcasted_iota(jnp.int32, (nbp, width), 0)
        eb_b = lax.broadcasted_iota(jnp.int32, (nbp, width), 1) >> 6
        expand = jnp.where(eb_m == 2 * kb0 + eb_b, 1.0, 0.0).astype(BF16)
        member = _dot(sel, expand)
        d = key_offsets(width) - kb0 * KEY_BLOCK
        causal = d >= 0
        valid = [causal & (member[g * tqs:g * tqs + tq] > 0.5) for g in range(NSA_KV)]
        attend(k_t, v_t, d.astype(F32), lambda r: valid[r // (NSA_HEADS // NSA_KV)])

    def sel_body(c, carry):
        sel_step(c * sel_group, *sel_chunk(c))
        return carry

    reset()
    lax.fori_loop(0, n_sel_chunks, sel_body, 0)
    if sel_tail is not None:
        sel_step(sel_tail[2], sel_tail[0], sel_tail[1])
    finish(1)

    reset()
    w_first = jnp.maximum(t0 - (WINDOW - 1) - win_start, 0) // KEY_BLOCK
    k0 = pl.multiple_of(w_first * KEY_BLOCK, KEY_BLOCK)
    d = key_offsets(WIN_SPAN) - (win_start + w_first * KEY_BLOCK)
    valid_w = (d >= 0) & (d < WINDOW)
    attend(kw_ref[0, :, pl.ds(k0, WIN_SPAN)], vw_ref[0, :, pl.ds(k0, WIN_SPAN)], d.astype(F32), lambda r: valid_w)
    finish(2)

    for ti in range(4):
        o_ref[0, :, ti * LANES:(ti + 1) * LANES] = jnp.where(lo, out_ref[rows(ti), :], out_ref[rows(4 + ti), :])


def _nsa_kernel(q_ref, g_ref, c_ref, ks_ref, vs_ref, kw_ref, vw_ref, o_ref, *scratch, tq, group, **static):
    i = pl.program_id(1)
    width = group * KEY_BLOCK

    def sel_chunk(c):
        k0 = pl.multiple_of(c * width, width)
        return ks_ref[0, :, pl.ds(k0, width)], vs_ref[0, :, pl.ds(k0, width)]

    _nsa_core(q_ref, g_ref, c_ref, kw_ref, vw_ref, o_ref, *scratch, i=i, sel_chunk=sel_chunk, sel_group=group,
              n_sel_chunks=((i + 1) * tq - 1) // width + 1, sel_tail=None, tq=tq, **static)


def _nsa_paged_kernel(pt_ref, cache_ref, q_ref, g_ref, c_ref, tail_ref, kw_ref, vw_ref, o_ref, buf_ref, sem_ref,
                      *scratch, group, **static):
    n_pages = buf_ref.shape[1]
    slot = _stream_pages(pt_ref, cache_ref, buf_ref, sem_ref, pl.program_id(0), pl.num_programs(0),
                         lambda step: (step, 0))

    def sel_chunk(c):
        pages = [buf_ref[slot, c * group + j] for j in range(group)]
        return (jnp.concatenate([p[0:LANES] for p in pages], axis=1),
                jnp.concatenate([p[LANES:2 * LANES] for p in pages], axis=1))

    _nsa_core(q_ref, g_ref, c_ref, kw_ref, vw_ref, o_ref, *scratch, i=0, sel_chunk=sel_chunk, sel_group=group,
              n_sel_chunks=n_pages // group,
              sel_tail=(tail_ref[0, 0:LANES, :], tail_ref[0, LANES:2 * LANES, :], n_pages), **static)


NSA_GROUP_ROWS = 4 * NSA_HEADS * KEY_BLOCK
WIN_SPAN = WINDOW + KEY_BLOCK


def _nsa_statics(tq, q_start, win_start, n_keys, key_blocks, win_keys):
    group = min(NSA_GROUP_ROWS // (NSA_HEADS * tq), 16, key_blocks)
    assert key_blocks % group == 0 and win_keys >= WIN_SPAN and tq <= KEY_BLOCK
    assert q_start % KEY_BLOCK == 0 and win_start % KEY_BLOCK == 0
    n_cmp = (n_keys - CMP_LEN) // CMP_STRIDE + 1
    n_blk = -(-n_keys // SEL_BLOCK)
    nbp = -(-n_blk // LANES) * LANES
    tqs = max(tq, LANES)
    scratch = [pltpu.VMEM((NSA_HEADS * tq, LANES), F32),
               pltpu.VMEM((NSA_HEADS * tq, LANES), F32),
               pltpu.VMEM((NSA_HEADS * tq, LANES), F32),
               pltpu.VMEM((NSA_HEADS * tq, LANES), F32),
               pltpu.VMEM((nbp, 2 * tqs), F32)]
    return dict(tq=tq, group=group, q_start=q_start, win_start=win_start, n_cmp=n_cmp, n_blk=n_blk), scratch


def _nsa_attention(q, gates, cmp, sel_kv, win_kv):
    batch, tokens, _ = q.shape
    tq = min(tokens, KEY_BLOCK)
    ncp = cmp.shape[1]
    static, scratch = _nsa_statics(tq, 0, 0, tokens, tokens // KEY_BLOCK, tokens)
    return pl.pallas_call(
        functools.partial(_nsa_kernel, **static),
        grid=(batch, tokens // tq),
        in_specs=[pl.BlockSpec((1, tq, 512), lambda b, i: (b, i, 0)),
                  pl.BlockSpec((1, tq, LANES), lambda b, i: (b, i, 0)),
                  pl.BlockSpec((1, ncp, 256), lambda b, i: (b, 0, 0)),
                  pl.BlockSpec((1, LANES, tokens), lambda b, i: (b, 0, 0)),
                  pl.BlockSpec((1, LANES, tokens), lambda b, i: (b, 1, 0)),
                  pl.BlockSpec((1, LANES, tokens), lambda b, i: (b, 0, 0)),
                  pl.BlockSpec((1, LANES, tokens), lambda b, i: (b, 1, 0))],
        out_specs=pl.BlockSpec((1, tq, 512), lambda b, i: (b, i, 0)),
        out_shape=jax.ShapeDtypeStruct((batch, tokens, 512), F32),
        scratch_shapes=scratch,
        compiler_params=_params(("parallel", "arbitrary")),
        name="nsa_attention",
    )(q, gates, cmp, sel_kv, sel_kv, win_kv, win_kv)


def _nsa_attention_paged(q, gates, cmp, page_table, sel_cache, sel_tail, win_kv, *, win_start):
    batch, tokens, _ = q.shape
    n_pages = page_table.shape[1]
    ncp, twp = cmp.shape[1], win_kv.shape[2]
    static, scratch = _nsa_statics(tokens, n_pages * PAGE, win_start, n_pages * PAGE + tokens, n_pages, twp)
    in_specs = [pl.BlockSpec((1, tokens, 512), lambda b, *_: (b, 0, 0)),
                pl.BlockSpec((1, tokens, LANES), lambda b, *_: (b, 0, 0)),
                pl.BlockSpec((1, ncp, 256), lambda b, *_: (b, 0, 0)),
                pl.BlockSpec((1, 256, PAGE), lambda b, *_: (b, 0, 0)),
                pl.BlockSpec((1, LANES, twp), lambda b, *_: (b, 0, 0)),
                pl.BlockSpec((1, LANES, twp), lambda b, *_: (b, 1, 0))]
    return _paged_call(functools.partial(_nsa_paged_kernel, **static), (batch,), page_table, sel_cache,
                       (q, gates, cmp, sel_tail, win_kv, win_kv), in_specs,
                       pl.BlockSpec((1, tokens, 512), lambda b, *_: (b, 0, 0)),
                       jax.ShapeDtypeStruct((batch, tokens, 512), F32), n_pages, scratch, "nsa_attention_paged")


SB_GROUP = 4
SB_PAGE_CHUNK = 16


def _sb_queries(q):
    lo = lax.broadcasted_iota(jnp.int32, q.shape, 1) < HEAD_DIM
    q = q * ATTN_SCALE
    return jnp.concatenate([jnp.where(lo, q, 0.0), jnp.where(lo, 0.0, q)], axis=0).astype(BF16)


def _sb_later_and_total():
    u_r = lax.broadcasted_iota(jnp.int32, (2 * KEY_BLOCK, 2 * KEY_BLOCK), 0) & (KEY_BLOCK - 1)
    u_c = lax.broadcasted_iota(jnp.int32, (2 * KEY_BLOCK, 2 * KEY_BLOCK), 1)
    return jnp.where((u_c >= KEY_BLOCK) | (u_r > u_c), 1.0, 0.0).astype(BF16)


def _sb_chunk(qs, k_t, v_t, mask, later_and_total, acc_ref, r_ref):
    width = k_t.shape[1]
    z = _dot(qs, k_t.astype(BF16))
    sp = jnp.maximum(z, 0.0) + jnp.log(1.0 + jnp.exp(-jnp.abs(z)))
    if mask is not None:
        sp = jnp.where(mask, sp, 0.0)
    hi, lo_part = _split_bf16(sp)
    run = r_ref[...]
    laters = []
    for g in reversed(range(width // KEY_BLOCK)):
        cols = slice(g * KEY_BLOCK, (g + 1) * KEY_BLOCK)
        sums = _dot(jnp.concatenate([hi[:, cols], lo_part[:, cols]], axis=1), later_and_total)
        laters.append(sums[:, :KEY_BLOCK] + run)
        run = run + sums[:, KEY_BLOCK:]
    later = laters[0] if len(laters) == 1 else jnp.concatenate(laters[::-1], axis=1)
    a = jnp.exp(z - sp - later)
    if mask is not None:
        a = jnp.where(mask, a, 0.0)
    acc_ref[...] += _dot_nt(a.astype(BF16), v_t.astype(BF16))
    r_ref[...] = run


def _sb_causal(rows, tq, width, q_pos0, k_pos0):
    q_idx = lax.broadcasted_iota(jnp.int32, (rows, width), 0)
    q_idx = jnp.where(q_idx >= tq, q_idx - tq, q_idx)
    return k_pos0 + lax.broadcasted_iota(jnp.int32, (rows, width), 1) < q_pos0 + q_idx


def _sb_kernel(q_ref, k_ref, v_ref, o_ref, acc_ref, r_ref, *, tq, group):
    i = pl.program_id(2)
    width = group * KEY_BLOCK
    qs = _sb_queries(q_ref[0])
    lt = _sb_later_and_total()

    def chunk(c, mask):
        k0 = pl.multiple_of(c * width, width)
        _sb_chunk(qs, k_ref[0, :, pl.ds(k0, width)], v_ref[0, :, pl.ds(k0, width)], mask, lt, acc_ref, r_ref)

    acc_ref[...] = jnp.zeros_like(acc_ref)
    r_ref[...] = jnp.zeros_like(r_ref)
    c_top = (i * tq) // width
    chunk(c_top, _sb_causal(2 * tq, tq, width, i * tq, c_top * width))

    def past_body(s, carry):
        chunk(c_top - 1 - s, None)
        return carry

    lax.fori_loop(0, c_top, past_body, 0)
    lo = lax.broadcasted_iota(jnp.int32, (tq, LANES), 1) < HEAD_DIM
    o_ref[0] = jnp.where(lo, acc_ref[0:tq, :], acc_ref[tq:2 * tq, :])


def _sb_attention(q, kv):
    batch, tokens, _ = q.shape
    tq = min(tokens, KEY_BLOCK)
    group = min(SB_GROUP, tokens // KEY_BLOCK)
    assert tokens % (group * KEY_BLOCK) == 0
    pairs = SB_HEADS // 2
    return pl.pallas_call(
        functools.partial(_sb_kernel, tq=tq, group=group),
        grid=(batch, pairs, tokens // tq),
        in_specs=[pl.BlockSpec((1, tq, LANES), lambda b, p, i: (b, i, p)),
                  pl.BlockSpec((1, LANES, tokens), lambda b, p, i: (b, p, 0)),
                  pl.BlockSpec((1, LANES, tokens), lambda b, p, i: (b, pairs + p, 0))],
        out_specs=pl.BlockSpec((1, tq, LANES), lambda b, p, i: (b, i, p)),
        out_shape=jax.ShapeDtypeStruct((batch, tokens, 512), F32),
        scratch_shapes=[pltpu.VMEM((2 * tq, LANES), F32), pltpu.VMEM((2 * tq, LANES), F32)],
        compiler_params=_params(("parallel", "parallel", "arbitrary")),
        name="stick_breaking_attention",
    )(q, kv, kv)


def _sb_paged_kernel(pt_ref, cache_ref, q_ref, tail_ref, o_ref, buf_ref, sem_ref, acc_ref, r_ref, *, tq, n_pages):
    pairs = SB_HEADS // 2
    chunk_pages = buf_ref.shape[1]
    n_chunks = n_pages // chunk_pages
    c = pl.program_id(1)
    step = pl.program_id(0) * n_chunks + c
    slot = _stream_pages(pt_ref, cache_ref, buf_ref, sem_ref, step, pl.num_programs(0) * n_chunks,
                         lambda s: (s // n_chunks, n_pages - (s % n_chunks + 1) * chunk_pages))
    qs = [_sb_queries(q_ref[0, :, p * LANES:(p + 1) * LANES]) for p in range(pairs)]
    lt = _sb_later_and_total()

    def k_rows(p):
        return slice(p * LANES, (p + 1) * LANES)

    def v_rows(p):
        return slice((pairs + p) * LANES, (pairs + p + 1) * LANES)

    @pl.when(c == 0)
    def _():
        acc_ref[...] = jnp.zeros_like(acc_ref)
        r_ref[...] = jnp.zeros_like(r_ref)
        mask = _sb_causal(2 * tq, tq, PAGE, 0, 0)
        for p in range(pairs):
            _sb_chunk(qs[p], tail_ref[0, k_rows(p), :], tail_ref[0, v_rows(p), :], mask, lt,
                      acc_ref.at[p], r_ref.at[p])

    for p in range(pairs):
        k_t = jnp.concatenate([buf_ref[slot, j, k_rows(p), :] for j in range(chunk_pages)], axis=1)
        v_t = jnp.concatenate([buf_ref[slot, j, v_rows(p), :] for j in range(chunk_pages)], axis=1)
        _sb_chunk(qs[p], k_t, v_t, None, lt, acc_ref.at[p], r_ref.at[p])

    @pl.when(c == n_chunks - 1)
    def _():
        lo = lax.broadcasted_iota(jnp.int32, (tq, LANES), 1) < HEAD_DIM
        for p in range(pairs):
            o_ref[0, :, p * LANES:(p + 1) * LANES] = jnp.where(lo, acc_ref[p, 0:tq, :], acc_ref[p, tq:2 * tq, :])


def _sb_attention_paged(q, page_table, cache, tail):
    batch, tokens, _ = q.shape
    n_pages = page_table.shape[1]
    chunk_pages = min(SB_PAGE_CHUNK, n_pages)
    assert n_pages % chunk_pages == 0
    pairs = SB_HEADS // 2
    in_specs = [pl.BlockSpec((1, tokens, 512), lambda b, c, *_: (b, 0, 0)),
                pl.BlockSpec((1, 1024, PAGE), lambda b, c, *_: (b, 0, 0))]
    return _paged_call(functools.partial(_sb_paged_kernel, tq=tokens, n_pages=n_pages),
                       (batch, n_pages // chunk_pages), page_table, cache, (q, tail), in_specs,
                       pl.BlockSpec((1, tokens, 512), lambda b, c, *_: (b, 0, 0)),
                       jax.ShapeDtypeStruct((batch, tokens, 512), F32), chunk_pages,
                       [pltpu.VMEM((pairs, 2 * tokens, LANES), F32), pltpu.VMEM((pairs, 2 * tokens, LANES), F32)],
                       "stick_breaking_attention_paged")


def _prep_weights(ffn1_w_in, ffn1_w_out, ffn2_w_in, ffn2_w_out, w_in, nsa_q_norm, nsa_k_norm,
                  cmp_pe, cmp_w1, cmp_w2, w_up_nsa, w_up_sb, w_out):
    w_perm = jnp.concatenate([w_in[:, _QPERM], w_in[:, 1304:1816], w_in[:, 2840:4888], w_in[:, 1280:1304],
                              jnp.zeros((D_MODEL, _C_END - _C_GN - 24), F32)], axis=1).astype(BF16)
    w_kv_t = jnp.concatenate([w_in[:, 512:1280], w_in[:, 1816:2840]], axis=1).T.astype(BF16)
    tile2 = lambda g: jnp.tile(g, 2).reshape(1, LANES)
    eye2 = jnp.eye(NSA_KV, dtype=F32)

    def big_w1(kv, half):
        w = cmp_w1[kv, half * CMP_STRIDE:(half + 1) * CMP_STRIDE]
        full = jnp.einsum('lde,gf->lgdfe', w, eye2)
        return full.reshape(CMP_STRIDE * LANES, LANES).astype(BF16)

    def pe_row(kv, half):
        pe = cmp_pe[kv, half * CMP_STRIDE:(half + 1) * CMP_STRIDE]
        return jnp.tile(pe[:, None, :], (1, NSA_KV, 1)).reshape(1, CMP_STRIDE * LANES)

    order = [(kv, half) for kv in range(2) for half in range(2)]
    w2_big = jnp.stack([jnp.einsum('de,gf->gdfe', cmp_w2[kv], eye2).reshape(LANES, LANES) for kv in range(2)])
    return dict(
        f1_in=ffn1_w_in.astype(BF16), f1_out=ffn1_w_out.astype(BF16),
        f2_in=ffn2_w_in.astype(BF16), f2_out=ffn2_w_out.astype(BF16),
        w_perm=w_perm, w_kv_t=w_kv_t, q_gain=tile2(nsa_q_norm),
        k_gain_cmp=tile2(nsa_k_norm[0]),
        k_gain2=jnp.stack([jnp.tile(nsa_k_norm[1], 2), jnp.tile(nsa_k_norm[2], 2)]).reshape(2, LANES, 1),
        pe4=jnp.concatenate([pe_row(kv, half) for kv, half in order], axis=0),
        w1_big=jnp.stack([big_w1(kv, half) for kv, half in order]), w2_big=w2_big.astype(BF16),
        w_up_nsa=w_up_nsa[_QPERM].astype(BF16), w_up_sb=w_up_sb.astype(BF16), w_out=w_out.astype(BF16))


def _pad_tokens(x, n):
    return jnp.pad(x, ((0, 0), (0, 0), (0, n - x.shape[2])))


def _per_sequence(x_t, batch, tokens):
    if x_t.shape[0] == batch:
        return x_t
    return x_t.reshape(x_t.shape[1], batch, tokens).transpose(1, 0, 2)


def _layer(x, mod, past, norms, w):
    batch, tokens, _ = x.shape
    n1, nm, n2 = norms
    x = _ffn(x, mod, 0, n1, w['f1_in'], w['f1_out'])
    q, sq, g_mn, g_ms, g_nsa, new_cmp, new_sel, win_rows, new_sb = _projection(
        x, mod, nm, w['w_perm'], w['w_kv_t'], w['q_gain'], w['k_gain2'])
    new_cmp, new_sel, win_rows, new_sb = (_per_sequence(a, batch, tokens) for a in (new_cmp, new_sel, win_rows, new_sb))
    cmp_w = (w['pe4'], w['w1_big'], w['w2_big'], w['k_gain_cmp'])
    if past is None:
        assert tokens % LANES == 0
        new_win = win_rows[:, :, tokens - min(WINDOW, tokens):]
        cmp = _compress(new_cmp, *cmp_w)
        o_nsa = _nsa_attention(q, g_nsa, cmp, new_sel, win_rows)
        o_sb = _sb_attention(sq, new_sb)
    else:
        page_table, cmp_cache, sel_cache, sb_cache, win_buf = past
        q_start = page_table.shape[1] * PAGE
        assert tokens < PAGE and (q_start + tokens - CMP_LEN) // CMP_STRIDE + 2 == q_start // CMP_STRIDE
        win_cat = jnp.concatenate([win_buf, win_rows], axis=2)
        new_win = win_cat[:, :, win_cat.shape[2] - min(WINDOW, win_cat.shape[2]):]
        win_all = _pad_tokens(win_cat, -(-win_cat.shape[2] // KEY_BLOCK) * KEY_BLOCK)
        cmp = _compress_paged(page_table, cmp_cache, *cmp_w)
        o_nsa = _nsa_attention_paged(q, g_nsa, cmp, page_table, sel_cache, _pad_tokens(new_sel, PAGE), win_all,
                                     win_start=q_start - win_buf.shape[2])
        o_sb = _sb_attention_paged(sq, page_table, sb_cache, _pad_tokens(new_sb, PAGE))
    x = _merge(x, mod, o_nsa, o_sb, g_mn, g_ms, w['w_up_nsa'], w['w_up_sb'], w['w_out'])
    x = _ffn(x, mod, 6, n2, w['f2_in'], w['f2_out'])
    return x, new_cmp, new_sel, new_sb, new_win


def kernel(x_prompt, x_sample, c_prompt, c_sample, cache_nsa_cmp, cache_nsa_sel, cache_sb, state_nsa_win, page_table, norm_ffn1, norm_mix, norm_ffn2, w_ada, b_ada, ffn1_w_in, ffn1_w_out, ffn2_w_in, ffn2_w_out, w_in, nsa_q_norm, nsa_k_norm, cmp_pe, cmp_w1, cmp_w2, w_up_nsa, w_up_sb, w_out):
    depth = w_in.shape[0]
    assert depth == 1, "single-layer trunk"
    bp, tp, _ = x_prompt.shape
    bs, ts, _ = x_sample.shape
    n_pool = cache_nsa_cmp.shape[1]
    w = _prep_weights(ffn1_w_in[0], ffn1_w_out[0], ffn2_w_in[0], ffn2_w_out[0], w_in[0], nsa_q_norm[0],
                      nsa_k_norm[0], cmp_pe[0], cmp_w1[0], cmp_w2[0], w_up_nsa[0], w_up_sb[0], w_out[0])
    norms = (norm_ffn1[0], norm_mix[0], norm_ffn2[0])
    mod = _modulation(jnp.concatenate([c_prompt, c_sample], axis=0), w_ada[0], b_ada[0])
    mod_p = mod[:bp].reshape(bp, 1, 9 * D_MODEL)
    mod_s = mod[bp:].reshape(bs, 1, 9 * D_MODEL)

    yp, cmp_p, sel_p, sb_p, win_p = _layer(x_prompt, mod_p, None, norms, w)

    def pages_t(cache):
        return cache[0].transpose(0, 2, 3, 4, 1).reshape(n_pool, -1, PAGE)

    win_buf = state_nsa_win[0].transpose(0, 2, 3, 4, 1).reshape(bs, 256, state_nsa_win.shape[2])
    past = (page_table, pages_t(cache_nsa_cmp), pages_t(cache_nsa_sel), pages_t(cache_sb), win_buf)
    ys, cmp_s, sel_s, sb_s, win_s = _layer(x_sample, mod_s, past, norms, w)

    def rows(a, heads):
        return a.reshape(1, a.shape[0], 2, heads, HEAD_DIM, a.shape[2]).transpose(0, 1, 5, 2, 3, 4)

    return (yp, ys, rows(cmp_p, NSA_KV), rows(sel_p, NSA_KV), rows(sb_p, SB_HEADS), rows(win_p, NSA_KV),
            rows(cmp_s, NSA_KV), rows(sel_s, NSA_KV), rows(sb_s, SB_HEADS), rows(win_s, NSA_KV))
```
